```python
import math
import jax, jax.numpy as jnp
from jax import lax
import numpy as np

D_MODEL = 1024
BATCH = 8
SEQ = 4096
DEPTH = 4

HEAD_DIM = 64
SC_WIDTH = 512
SC_TAPS = 3
DA_HEADS = 8
DA_WIDTH = DA_HEADS * HEAD_DIM
DA_PAIRS = ((128, 1), (512, 4), (2048, 16))
CC_WIDTH = 512
CC_TAPS = 31
SB_HEADS = 8
SB_WIDTH = SB_HEADS * HEAD_DIM
BLOCK_Q = 128
D_FF = 2816
FFN_TAPS = 3
REL_BUCKETS = 32
REL_MAX_DIST = 2048
EPS = 1e-6

N_EVEN = (DEPTH + 1) // 2
N_ODD = DEPTH // 2
EVEN_IN = 3 * SC_WIDTH + 3 * DA_WIDTH
EVEN_MIX = SC_WIDTH + DA_WIDTH
ODD_IN = 2 * CC_WIDTH + 3 * SB_WIDTH
ODD_MIX = CC_WIDTH + SB_WIDTH

kernel_name = "hybrid_shortconv_dilated_conformer_stickbreak"


def rms_norm(x, g):
    xf = x.astype(jnp.float32)
    y = xf * lax.rsqrt(jnp.mean(xf * xf, axis=-1, keepdims=True) + EPS)
    return (y * g.astype(jnp.float32)).astype(x.dtype)


def layer_norm(x, g, b):
    xf = x.astype(jnp.float32)
    mu = jnp.mean(xf, axis=-1, keepdims=True)
    var = jnp.mean(jnp.square(xf - mu), axis=-1, keepdims=True)
    y = (xf - mu) * lax.rsqrt(var + EPS) * g.astype(jnp.float32) + b.astype(jnp.float32)
    return y.astype(x.dtype)


def causal_dwconv(x, w):
    taps, c = w.shape
    return lax.conv_general_dilated(
        x, w[:, None, :].astype(x.dtype), window_strides=(1,),
        padding=((taps - 1, 0),), dimension_numbers=("NWC", "WIO", "NWC"),
        feature_group_count=c)


def t5_bucket(dist):
    max_exact = REL_BUCKETS // 2
    d = jnp.maximum(dist, 1).astype(jnp.float32)
    large = max_exact + (jnp.log(d / max_exact) / math.log(REL_MAX_DIST / max_exact)
                         * (REL_BUCKETS - max_exact)).astype(jnp.int32)
    large = jnp.minimum(large, REL_BUCKETS - 1)
    return jnp.where(dist < max_exact, dist, large)


def dilated_branch(q, k, v, span, dilation, rel_bias):
    b, s, h, dh = q.shape
    L = s // dilation
    n = b * dilation

    def to_sub(t):
        return jnp.swapaxes(t.reshape(b, L, dilation, h, dh), 1, 2).reshape(n, L, h, dh)

    def from_sub(t):
        t = t.reshape((b, dilation, L) + t.shape[3:])
        return jnp.swapaxes(t, 1, 2).reshape((b, s) + t.shape[3:])

    qs, ks, vs = to_sub(q), to_sub(k), to_sub(v)
    bq = math.gcd(BLOCK_Q, L)
    nb = L // bq
    kp = jnp.pad(ks, ((0, 0), (span, 0), (0, 0), (0, 0)))
    vp = jnp.pad(vs, ((0, 0), (span, 0), (0, 0), (0, 0)))
    idx = jnp.arange(nb)[:, None] * bq + jnp.arange(bq + span)[None, :]
    kb = kp[:, idx]
    vb = vp[:, idx]
    qb = qs.reshape(n, nb, bq, h, dh)
    logits = jnp.einsum("nbqhd,nbkhd->nbhqk", qb, kb) / math.sqrt(dh)
    rel = jnp.arange(bq)[:, None] - jnp.arange(bq + span)[None, :] + span
    valid = ((rel >= 0) & (rel <= span))[None] & ((idx - span) >= 0)[:, None, :]
    bias = rel_bias.astype(jnp.float32)[t5_bucket(jnp.clip(rel, 0, span) * dilation)]
    logits = logits + jnp.transpose(bias, (2, 0, 1))[None, None]
    logits = jnp.where(valid[None, :, None], logits, -jnp.inf)
    m = jnp.max(logits, axis=-1)
    p = jnp.exp(logits - m[..., None])
    den = jnp.sum(p, axis=-1)
    u = jnp.einsum("nbhqk,nbkhd->nbqhd", p, vb)
    m = jnp.transpose(m, (0, 1, 3, 2))
    den = jnp.transpose(den, (0, 1, 3, 2))
    return from_sub(u), from_sub(m), from_sub(den)


def stick_breaking(q, k, v):
    b, s, h, dh = q.shape
    nb = s // BLOCK_Q
    qb = jnp.moveaxis(q.reshape(b, nb, BLOCK_Q, h, dh), 1, 0)
    kpos = jnp.arange(s)

    def one_block(args):
        qi, start = args
        z = jnp.einsum("bqhd,bkhd->bhqk", qi, k) / math.sqrt(dh)
        qpos = start + jnp.arange(BLOCK_Q)
        before = (kpos[None, :] < qpos[:, None])[None, None]
        log_beta = jax.nn.log_sigmoid(z)
        log_rest = jnp.where(before, log_beta - z, 0.0)
        later = lax.cumsum(log_rest, axis=3, reverse=True) - log_rest
        a = jnp.where(before, jnp.exp(log_beta + later), 0.0)
        return jnp.einsum("bhqk,bkhd->bqhd", a, v)

    o = lax.map(one_block, (qb, jnp.arange(nb) * BLOCK_Q))
    return jnp.moveaxis(o, 0, 1).reshape(b, s, h, dh)


def even_mixer(h, w_in, w_out, w_sc, rel_bias):
    b, s, _ = h.shape
    p = h @ w_in
    o1 = SC_WIDTH
    o3 = 3 * SC_WIDTH
    gate_b, gate_c, xa, q, k, v = jnp.split(
        p, [o1, 2 * o1, o3, o3 + DA_WIDTH, o3 + 2 * DA_WIDTH], axis=-1)
    y_a = gate_b * causal_dwconv(gate_c * xa, w_sc)
    heads = lambda t: t.reshape(b, s, DA_HEADS, HEAD_DIM).astype(jnp.float32)
    q, k, v = heads(q), heads(k), heads(v)
    outs = [dilated_branch(q, k, v, win // dil, dil, rel_bias) for win, dil in DA_PAIRS]
    u = jnp.stack([o[0] for o in outs])
    m = jnp.stack([o[1] for o in outs])
    den = jnp.stack([o[2] for o in outs])
    wgt = jnp.exp(m - jnp.max(m, axis=0, keepdims=True))
    o = jnp.einsum("gbsh,gbshd->bshd", wgt, u) / jnp.sum(wgt * den, axis=0)[..., None]
    y_b = o.reshape(b, s, DA_WIDTH).astype(h.dtype)
    return jnp.concatenate([y_a, y_b], axis=-1) @ w_out


def odd_mixer(h, w_in, w_out, w_cc, b_cc, ln_g, ln_b):
    b, s, _ = h.shape
    p = h @ w_in
    c2 = 2 * CC_WIDTH
    a, gate, q, k, v = jnp.split(
        p, [CC_WIDTH, c2, c2 + SB_WIDTH, c2 + 2 * SB_WIDTH], axis=-1)
    g = a * jax.nn.sigmoid(gate)
    g = causal_dwconv(g, w_cc) + b_cc.astype(g.dtype)
    y_c = jax.nn.silu(layer_norm(g, ln_g, ln_b))
    heads = lambda t: t.reshape(b, s, SB_HEADS, HEAD_DIM).astype(jnp.float32)
    y_d = stick_breaking(heads(q), heads(k), heads(v)).reshape(b, s, SB_WIDTH).astype(h.dtype)
    return jnp.concatenate([y_c.astype(h.dtype), y_d], axis=-1) @ w_out


def conv_ffn(h, w_up, w_conv, b_conv, w_down):
    g, u = jnp.split(h @ w_up, 2, axis=-1)
    g = causal_dwconv(g, w_conv) + b_conv.astype(g.dtype)
    return (jax.nn.silu(g) * u) @ w_down


def setup_inputs(seed: int = 0) -> dict:
    key = jax.random.key(seed)
    ks = jax.random.split(key, 20)
    nrm = lambda k, shape, scale: jax.random.normal(k, shape, jnp.float32) * scale
    return {
        "x": nrm(ks[0], (BATCH, SEQ, D_MODEL), 1.0),
        "norm_g": 1.0 + nrm(ks[1], (DEPTH, 4, D_MODEL), 0.05),
        "rel_bias": nrm(ks[2], (REL_BUCKETS, DA_HEADS), 0.5),
        "w_in_even": nrm(ks[3], (N_EVEN, D_MODEL, EVEN_IN), D_MODEL ** -0.5),
        "w_out_even": nrm(ks[4], (N_EVEN, EVEN_MIX, D_MODEL), EVEN_MIX ** -0.5),
        "w_sc": nrm(ks[5], (N_EVEN, SC_TAPS, SC_WIDTH), SC_TAPS ** -0.5),
        "w_in_odd": nrm(ks[6], (N_ODD, D_MODEL, ODD_IN), D_MODEL ** -0.5),
        "w_out_odd": nrm(ks[7], (N_ODD, ODD_MIX, D_MODEL), ODD_MIX ** -0.5),
        "w_cc": nrm(ks[8], (N_ODD, CC_TAPS, CC_WIDTH), CC_TAPS ** -0.5),
        "b_cc": nrm(ks[9], (N_ODD, CC_WIDTH), 0.02),
        "ln_cc_g": 1.0 + nrm(ks[10], (N_ODD, CC_WIDTH), 0.05),
        "ln_cc_b": nrm(ks[11], (N_ODD, CC_WIDTH), 0.02),
        "w_up": nrm(ks[12], (DEPTH, D_MODEL, 2 * D_FF), D_MODEL ** -0.5),
        "w_ffn_conv": nrm(ks[13], (DEPTH, FFN_TAPS, D_FF), FFN_TAPS ** -0.5),
        "b_ffn_conv": nrm(ks[14], (DEPTH, D_FF), 0.02),
        "w_down": nrm(ks[15], (DEPTH, D_FF, D_MODEL), D_FF ** -0.5),
    }


def reference(x, norm_g, rel_bias, w_in_even, w_out_even, w_sc, w_in_odd, w_out_odd,
              w_cc, b_cc, ln_cc_g, ln_cc_b, w_up, w_ffn_conv, b_ffn_conv, w_down):
    for layer in range(DEPTH):
        h = rms_norm(x, norm_g[layer, 0])
        if layer % 2 == 0:
            i = layer // 2
            mix = even_mixer(h, w_in_even[i], w_out_even[i], w_sc[i], rel_bias)
        else:
            i = layer // 2
            mix = odd_mixer(h, w_in_odd[i], w_out_odd[i], w_cc[i], b_cc[i], ln_cc_g[i], ln_cc_b[i])
        x = x + rms_norm(mix, norm_g[layer, 1])
        h = rms_norm(x, norm_g[layer, 2])
        x = x + rms_norm(conv_ffn(h, w_up[layer], w_ffn_conv[layer], b_ffn_conv[layer], w_down[layer]),
                         norm_g[layer, 3])
    return x
```

```python
import functools
import math

import numpy as np
import jax
import jax.numpy as jnp
from jax import lax
from jax.experimental import pallas as pl
from jax.experimental.pallas import tpu as pltpu

F32 = jnp.float32
BF16 = jnp.bfloat16

EPS = 1e-6
HEAD_DIM = 64
LANES = 128
SUBLANES = 8
HEADS_PER_TILE = LANES // HEAD_DIM
DA_PAIRS = ((128, 1), (512, 4), (2048, 16))
DA_SPAN = 128
BLOCK = 128
REL_BUCKETS = 32
REL_MAX_DIST = 2048
MASKED = -1e30
SB_EXIT = -105.0
VMEM_LIMIT = 56 * 1024 * 1024


def _cparams(*sem):
    return pltpu.CompilerParams(dimension_semantics=sem, vmem_limit_bytes=VMEM_LIMIT)


def _rms(x, g):
    return x * lax.rsqrt(jnp.mean(x * x, axis=-1, keepdims=True) + EPS) * g


def _norm_matmul_kernel(x_ref, g_ref, w_ref, *o_refs, col_chunk):
    h = _rms(x_ref[...], g_ref[...]).astype(BF16)
    col = 0
    for o_ref in o_refs:
        width = o_ref.shape[1]
        for c in range(0, width, col_chunk):
            o_ref[:, c:c + col_chunk] = jnp.dot(
                h, w_ref[:, col + c:col + c + col_chunk],
                preferred_element_type=F32).astype(o_ref.dtype)
        col += width


def norm_matmul(x, g, w, outs, tm=512, col_chunk=512):
    n, d = x.shape
    assert n % tm == 0 and sum(wd for wd, _ in outs) == w.shape[1]
    assert all(wd % col_chunk == 0 for wd, _ in outs)
    return pl.pallas_call(
        functools.partial(_norm_matmul_kernel, col_chunk=col_chunk),
        grid=(n // tm,),
        in_specs=[pl.BlockSpec((tm, d), lambda i: (i, 0)),
                  pl.BlockSpec((1, d), lambda i: (0, 0)),
                  pl.BlockSpec(w.shape, lambda i: (0, 0))],
        out_specs=[pl.BlockSpec((tm, wd), lambda i: (i, 0)) for wd, _ in outs],
        out_shape=[jax.ShapeDtypeStruct((n, wd), dt) for wd, dt in outs],
        compiler_params=_cparams("parallel"),
        name="norm_in_proj",
    )(x, g.reshape(1, d), w)


def _short_conv_kernel(gb_ref, gc_ref, xa_ref, w_ref, o_ref, pad_ref):
    ts = o_ref.shape[0]
    @pl.when(pl.program_id(1) == 0)
    def _():
        pad_ref[0:SUBLANES, :] = jnp.zeros((SUBLANES, pad_ref.shape[1]), F32)

    @pl.when(pl.program_id(1) != 0)
    def _():
        pad_ref[0:SUBLANES, :] = pad_ref[ts:ts + SUBLANES, :]

    c = gc_ref[...].astype(F32) * xa_ref[...].astype(F32)
    pad_ref[SUBLANES:, :] = c
    conv = (w_ref[2:3, :] * c
            + w_ref[1:2, :] * pad_ref[SUBLANES - 1:SUBLANES - 1 + ts, :]
            + w_ref[0:1, :] * pad_ref[SUBLANES - 2:SUBLANES - 2 + ts, :])
    o_ref[...] = (gb_ref[...].astype(F32) * conv).astype(o_ref.dtype)


def short_conv(pa, w_sc, batch, seq, ts=512):
    n = pa.shape[0]
    width = w_sc.shape[1]
    nt = seq // ts
    return pl.pallas_call(
        _short_conv_kernel,
        grid=(batch, nt),
        in_specs=[pl.BlockSpec((ts, width), lambda b, t: (b * nt + t, 0)),
                  pl.BlockSpec((ts, width), lambda b, t: (b * nt + t, 1)),
                  pl.BlockSpec((ts, width), lambda b, t: (b * nt + t, 2)),
                  pl.BlockSpec(w_sc.shape, lambda b, t: (0, 0))],
        out_specs=pl.BlockSpec((ts, width), lambda b, t: (b * nt + t, 0)),
        out_shape=jax.ShapeDtypeStruct((n, width), BF16),
        scratch_shapes=[pltpu.VMEM((ts + SUBLANES, width), F32)],
        compiler_params=_cparams("arbitrary", "arbitrary"),
        name="short_conv",
    )(pa, pa, pa, w_sc)


def _bucket_tables():
    rel = np.arange(BLOCK)[:, None] - np.arange(2 * BLOCK)[None, :] + DA_SPAN
    valid = (rel >= 0) & (rel <= DA_SPAN)
    max_exact = REL_BUCKETS // 2
    tabs = []
    for _, dil in DA_PAIRS:
        dist = np.clip(rel, 0, DA_SPAN) * dil
        d = np.maximum(dist, 1).astype(np.float32)
        large = max_exact + (np.log(d / np.float32(max_exact))
                             / np.float32(math.log(REL_MAX_DIST / max_exact))
                             * np.float32(REL_BUCKETS - max_exact)).astype(np.int32)
        large = np.minimum(large, REL_BUCKETS - 1)
        bucket = np.where(dist < max_exact, dist, large)
        tabs.append(np.where(valid, bucket, -1))
    return np.stack(tabs).astype(np.int32)


def _bias_table_kernel(rb_ref, idx_ref, o_ref):
    h = pl.program_id(1)
    idx = idx_ref[0]
    acc = jnp.full(idx.shape, MASKED, F32)
    for b in range(REL_BUCKETS):
        acc = jnp.where(idx == b, rb_ref[b, h], acc)
    o_ref[0, 0] = acc


def bias_tables(rel_bias):
    idx = jnp.asarray(_bucket_tables())
    heads = rel_bias.shape[1]
    return pl.pallas_call(
        _bias_table_kernel,
        grid=(len(DA_PAIRS), heads),
        in_specs=[pl.BlockSpec(memory_space=pltpu.SMEM),
                  pl.BlockSpec((1, BLOCK, 2 * BLOCK), lambda g, h: (g, 0, 0))],
        out_specs=pl.BlockSpec((1, 1, BLOCK, 2 * BLOCK), lambda g, h: (g, h, 0, 0)),
        out_shape=jax.ShapeDtypeStruct((len(DA_PAIRS), heads, BLOCK, 2 * BLOCK), F32),
        compiler_params=_cparams("arbitrary", "arbitrary"),
        name="bias_tables",
    )(rel_bias, idx)


def _dilated_attn_kernel(q_ref, k_ref, v_ref, bias_ref, o_ref,
                         m0_ref, m1_ref, l0_ref, l1_ref, acc_ref):
    seq = q_ref.shape[0]
    lane = lax.broadcasted_iota(jnp.int32, (1, LANES), 1)
    head_mask = (lane < HEAD_DIM, lane >= HEAD_DIM)
    m_refs = (m0_ref, m1_ref)
    l_refs = (l0_ref, l1_ref)

    m0_ref[...] = jnp.full(m0_ref.shape, MASKED, F32)
    m1_ref[...] = jnp.full(m1_ref.shape, MASKED, F32)
    l0_ref[...] = jnp.zeros(l0_ref.shape, F32)
    l1_ref[...] = jnp.zeros(l1_ref.shape, F32)
    acc_ref[...] = jnp.zeros(acc_ref.shape, F32)

    nt_dims = (((1,), (1,)), ((), ()))
    for g, (_, dil) in enumerate(DA_PAIRS):
        nb = seq // dil // BLOCK

        def rows(start):
            if dil == 1:
                return pl.ds(pl.multiple_of(start, BLOCK), BLOCK)
            return pl.ds(start, BLOCK, stride=dil)

        def body(blk, carry):
            r = blk // nb
            j = blk % nb
            cur = rows(r + j * (BLOCK * dil))
            prev = rows(r + jnp.maximum(j - 1, 0) * (BLOCK * dil))
            pen = jnp.where(j == 0, MASKED, 0.0).astype(F32)
            q = q_ref[cur, :] * (1.0 / math.sqrt(HEAD_DIM))
            k_cur = k_ref[cur, :].astype(BF16)
            k_prev = k_ref[prev, :].astype(BF16)
            v_cur = v_ref[cur, :].astype(BF16)
            v_prev = v_ref[prev, :].astype(BF16)
            acc_old = acc_ref[cur, :]
            alphas = []
            pvs = []
            for h in range(HEADS_PER_TILE):
                qh = jnp.where(head_mask[h], q, 0.0).astype(BF16)
                s_prev = lax.dot_general(qh, k_prev, nt_dims, preferred_element_type=F32)
                s_cur = lax.dot_general(qh, k_cur, nt_dims, preferred_element_type=F32)
                s_prev = s_prev + bias_ref[g, h, :, 0:BLOCK] + pen
                s_cur = s_cur + bias_ref[g, h, :, BLOCK:2 * BLOCK]
                m_old = m_refs[h][cur, :]
                row_max = jnp.max(jnp.maximum(s_prev, s_cur), axis=-1, keepdims=True)
                m_new = jnp.maximum(m_old, row_max)
                alpha = jnp.exp(m_old - m_new)
                p_prev = jnp.exp(s_prev - m_new)
                p_cur = jnp.exp(s_cur - m_new)
                row_sum = jnp.sum(p_prev + p_cur, axis=-1, keepdims=True)
                l_refs[h][cur, :] = l_refs[h][cur, :] * alpha + row_sum
                m_refs[h][cur, :] = m_new
                pv = (jnp.dot(p_prev.astype(BF16), v_prev, preferred_element_type=F32)
                      + jnp.dot(p_cur.astype(BF16), v_cur, preferred_element_type=F32))
                alphas.append(alpha)
                pvs.append(pv)
            alpha = jnp.where(head_mask[0], alphas[0], alphas[1])
            pv = jnp.where(head_mask[0], pvs[0], pvs[1])
            acc_ref[cur, :] = acc_old * alpha + pv
            return carry

        lax.fori_loop(0, seq // BLOCK, body, 0)

    den = jnp.where(head_mask[0], l0_ref[...], l1_ref[...])
    o_ref[...] = (acc_ref[...] / den).astype(o_ref.dtype)


def dilated_attention(qkv, bias, batch, seq):
    n = qkv.shape[0]
    width = qkv.shape[1] // 3
    nhp = width // LANES
    blk = lambda off: pl.BlockSpec((seq, LANES), lambda b, p: (b, off + p))
    return pl.pallas_call(
        _dilated_attn_kernel,
        grid=(batch, nhp),
        in_specs=[blk(0), blk(nhp), blk(2 * nhp),
                  pl.BlockSpec((len(DA_PAIRS), HEADS_PER_TILE, BLOCK, 2 * BLOCK),
                               lambda b, p: (0, p, 0, 0))],
        out_specs=pl.BlockSpec((seq, LANES), lambda b, p: (b, p)),
        out_shape=jax.ShapeDtypeStruct((n, width), BF16),
        scratch_shapes=[pltpu.VMEM((seq, LANES), F32)] * 5,
        compiler_params=_cparams("parallel", "parallel"),
        name="dilated_attention",
    )(qkv, qkv, qkv, bias)


def _conformer_conv_kernel(a_ref, gate_ref, w_ref, b_ref, lg_ref, lb_ref, o_ref, pad_ref,
                           *, taps, halo, chunk):
    ts = o_ref.shape[0]

    @pl.when(pl.program_id(1) == 0)
    def _():
        pad_ref[0:halo, :] = jnp.zeros((halo, pad_ref.shape[1]), F32)

    @pl.when(pl.program_id(1) != 0)
    def _():
        pad_ref[0:halo, :] = pad_ref[ts:ts + halo, :]

    pad_ref[halo:, :] = a_ref[...].astype(F32) * jax.nn.sigmoid(gate_ref[...].astype(F32))

    lead = halo - (taps - 1)
    groups = -(-taps // SUBLANES)
    for c0 in range(0, ts, chunk):
        acc = jnp.zeros((chunk, o_ref.shape[1]), F32)
        for b in range(SUBLANES):
            n_a = len(range(b, taps, SUBLANES))
            win = pad_ref[c0 + lead + b:c0 + lead + b + chunk + SUBLANES * (n_a - 1), :]
            for a in range(n_a):
                j = SUBLANES * a + b
                acc = acc + w_ref[j:j + 1, :] * win[SUBLANES * a:SUBLANES * a + chunk, :]
        y = acc + b_ref[...]
        mu = jnp.mean(y, axis=-1, keepdims=True)
        yc = y - mu
        var = jnp.mean(yc * yc, axis=-1, keepdims=True)
        z = yc * lax.rsqrt(var + EPS) * lg_ref[...] + lb_ref[...]
        o_ref[c0:c0 + chunk, :] = (z * jax.nn.sigmoid(z)).astype(o_ref.dtype)
    del groups


def conformer_conv(p, w_cc, b_cc, ln_g, ln_b, batch, seq, ts=256, chunk=32):
    n = p.shape[0]
    taps, width = w_cc.shape
    halo = -(-(taps - 1) // SUBLANES) * SUBLANES
    nt = seq // ts
    vec = lambda v: v.reshape(1, width)
    vspec = pl.BlockSpec((1, width), lambda b, t: (0, 0))
    return pl.pallas_call(
        functools.partial(_conformer_conv_kernel, taps=taps, halo=halo, chunk=chunk),
        grid=(batch, nt),
        in_specs=[pl.BlockSpec((ts, width), lambda b, t: (b * nt + t, 0)),
                  pl.BlockSpec((ts, width), lambda b, t: (b * nt + t, 1)),
                  pl.BlockSpec(w_cc.shape, lambda b, t: (0, 0)),
                  vspec, vspec, vspec],
        out_specs=pl.BlockSpec((ts, width), lambda b, t: (b * nt + t, 0)),
        out_shape=jax.ShapeDtypeStruct((n, width), BF16),
        scratch_shapes=[pltpu.VMEM((ts + halo, width), F32)],
        compiler_params=_cparams("arbitrary", "arbitrary"),
        name="conformer_conv",
    )(p, p, w_cc, vec(b_cc), vec(ln_g), vec(ln_b))


def _suffix_sum_matrix():
    j = np.arange(BLOCK)[:, None]
    s = np.arange(BLOCK)[None, :]
    half = np.concatenate([(j > s).astype(np.float32), np.ones((BLOCK, BLOCK), np.float32)], axis=1)
    return np.concatenate([half, half], axis=0)


def _stick_breaking_kernel(q_ref, k_ref, v_ref, u_ref, o_ref):
    qb = pl.program_id(2)
    lane = lax.broadcasted_iota(jnp.int32, (1, LANES), 1)
    head_mask = (lane < HEAD_DIM, lane >= HEAD_DIM)
    row = lax.broadcasted_iota(jnp.int32, (BLOCK, BLOCK), 0)
    col = lax.broadcasted_iota(jnp.int32, (BLOCK, BLOCK), 1)
    before = col < row
    nt_dims = (((1,), (1,)), ((), ()))
    q = q_ref[...].astype(F32) * (1.0 / math.sqrt(HEAD_DIM))

    def block_terms(qh, kb):
        ks = pl.ds(pl.multiple_of(kb * BLOCK, BLOCK), BLOCK)
        z = lax.dot_general(qh, k_ref[ks, :], nt_dims, preferred_element_type=F32)
        soft = jnp.log(1.0 + jnp.exp(-jnp.abs(z)))
        log_beta = jnp.minimum(z, 0.0) - soft
        log_rest = log_beta - z
        return log_beta, log_rest, ks

    def accumulate(log_beta, log_rest, ks, run, acc, mask):
        hi = log_rest.astype(BF16)
        lo = (log_rest - hi.astype(F32)).astype(BF16)
        sums = jnp.dot(jnp.concatenate([hi, lo], axis=1), u_ref[...],
                       preferred_element_type=F32)
        a = jnp.exp(log_beta + sums[:, 0:BLOCK] + run)
        if mask is not None:
            a = jnp.where(mask, a, 0.0)
        acc = acc + jnp.dot(a.astype(BF16), v_ref[ks, :], preferred_element_type=F32)
        return run + sums[:, BLOCK:2 * BLOCK], acc

    outs = []
    for h in range(HEADS_PER_TILE):
        qh = jnp.where(head_mask[h], q, 0.0).astype(BF16)
        log_beta, log_rest, ks = block_terms(qh, qb)
        log_rest = jnp.where(before, log_rest, 0.0)
        run, acc = accumulate(log_beta, log_rest, ks, jnp.zeros((BLOCK, LANES), F32),
                              jnp.zeros((BLOCK, LANES), F32), before)

        def cond(c):
            kb, run, _ = c
            return jnp.logical_and(kb >= 0, jnp.max(run) > SB_EXIT)

        def body(c):
            kb, run, acc = c
            log_beta, log_rest, ks = block_terms(qh, kb)
            run, acc = accumulate(log_beta, log_rest, ks, run, acc, None)
            return kb - 1, run, acc

        _, _, acc = lax.while_loop(cond, body, (qb - 1, run, acc))
        outs.append(acc)
    o_ref[...] = jnp.where(head_mask[0], outs[0], outs[1]).astype(o_ref.dtype)


def stick_breaking(p, col0, batch, seq):
    n = p.shape[0]
    width = (p.shape[1] - col0) // 3
    nhp = width // LANES
    c0 = col0 // LANES
    nq = seq // BLOCK
    u = jnp.asarray(_suffix_sum_matrix(), BF16)
    return pl.pallas_call(
        _stick_breaking_kernel,
        grid=(batch, nhp, nq),
        in_specs=[pl.BlockSpec((BLOCK, LANES), lambda b, p_, i: (b * nq + i, c0 + p_)),
                  pl.BlockSpec((seq, LANES), lambda b, p_, i: (b, c0 + nhp + p_)),
                  pl.BlockSpec((seq, LANES), lambda b, p_, i: (b, c0 + 2 * nhp + p_)),
                  pl.BlockSpec(u.shape, lambda b, p_, i: (0, 0))],
        out_specs=pl.BlockSpec((BLOCK, LANES), lambda b, p_, i: (b * nq + i, p_)),
        out_shape=jax.ShapeDtypeStruct((n, width), BF16),
        compiler_params=_cparams("parallel", "parallel", "arbitrary"),
        name="stick_breaking",
    )(p, p, p, u)


def _out_proj_kernel(ya_ref, yb_ref, wa_ref, wb_ref, g_ref, x_ref, o_ref):
    mix = (jnp.dot(ya_ref[...], wa_ref[...], preferred_element_type=F32)
           + jnp.dot(yb_ref[...], wb_ref[...], preferred_element_type=F32))
    o_ref[...] = x_ref[...] + _rms(mix, g_ref[...])


def out_proj(ya, yb, w_out, g, x, tm=512):
    n, d = x.shape
    wa_rows = ya.shape[1]
    wb_rows = yb.shape[1]
    return pl.pallas_call(
        _out_proj_kernel,
        grid=(n // tm,),
        in_specs=[pl.BlockSpec((tm, wa_rows), lambda i: (i, 0)),
                  pl.BlockSpec((tm, wb_rows), lambda i: (i, 0)),
                  pl.BlockSpec((wa_rows, d), lambda i: (0, 0)),
                  pl.BlockSpec((wb_rows, d), lambda i: (wa_rows // wb_rows, 0)),
                  pl.BlockSpec((1, d), lambda i: (0, 0)),
                  pl.BlockSpec((tm, d), lambda i: (i, 0))],
        out_specs=pl.BlockSpec((tm, d), lambda i: (i, 0)),
        out_shape=jax.ShapeDtypeStruct((n, d), F32),
        compiler_params=_cparams("parallel"),
        name="out_proj",
    )(ya, yb, w_out, w_out, g.reshape(1, d), x)


def _conv_ffn_kernel(x_ref, gin_ref, wg_ref, wu_ref, wc_ref, bc_ref, wd_ref, gout_ref, o_ref,
                     h_ref, acc_ref, pad_ref, carry_ref, *, tiles_per_seq):
    i = pl.program_id(0)
    c = pl.program_id(1)
    tm = x_ref.shape[0]

    @pl.when(c == 0)
    def _():
        h_ref[...] = _rms(x_ref[...], gin_ref[...]).astype(BF16)

    h = h_ref[...]
    gate = jnp.dot(h, wg_ref[...], preferred_element_type=F32)
    up = jnp.dot(h, wu_ref[...], preferred_element_type=F32)

    @pl.when(i % tiles_per_seq == 0)
    def _():
        pad_ref[0:SUBLANES, :] = jnp.zeros((SUBLANES, pad_ref.shape[1]), F32)

    @pl.when(i % tiles_per_seq != 0)
    def _():
        pad_ref[0:SUBLANES, :] = carry_ref[c]

    pad_ref[SUBLANES:, :] = gate
    carry_ref[c] = gate[tm - SUBLANES:, :]
    conv = (wc_ref[2:3, :] * gate
            + wc_ref[1:2, :] * pad_ref[SUBLANES - 1:SUBLANES - 1 + tm, :]
            + wc_ref[0:1, :] * pad_ref[SUBLANES - 2:SUBLANES - 2 + tm, :]
            + bc_ref[...])
    act = (conv * jax.nn.sigmoid(conv) * up).astype(BF16)
    part = jnp.dot(act, wd_ref[...], preferred_element_type=F32)

    @pl.when(c == 0)
    def _():
        acc_ref[...] = part

    @pl.when(c != 0)
    def _():
        acc_ref[...] += part

    @pl.when(c == pl.num_programs(1) - 1)
    def _():
        o_ref[...] = x_ref[...] + _rms(acc_ref[...], gout_ref[...])


def conv_ffn(x, g_in, w_up, w_conv, b_conv, w_down, g_out, seq, tm=1024, fc=256):
    n, d = x.shape
    d_ff = w_down.shape[0]
    assert d_ff % fc == 0 and seq % tm == 0
    nc = d_ff // fc
    return pl.pallas_call(
        functools.partial(_conv_ffn_kernel, tiles_per_seq=seq // tm),
        grid=(n // tm, nc),
        in_specs=[pl.BlockSpec((tm, d), lambda i, c: (i, 0)),
                  pl.BlockSpec((1, d), lambda i, c: (0, 0)),
                  pl.BlockSpec((d, fc), lambda i, c: (0, c)),
                  pl.BlockSpec((d, fc), lambda i, c: (0, nc + c)),
                  pl.BlockSpec((w_conv.shape[0], fc), lambda i, c: (0, c)),
                  pl.BlockSpec((1, fc), lambda i, c: (0, c)),
                  pl.BlockSpec((fc, d), lambda i, c: (c, 0)),
                  pl.BlockSpec((1, d), lambda i, c: (0, 0))],
        out_specs=pl.BlockSpec((tm, d), lambda i, c: (i, 0)),
        out_shape=jax.ShapeDtypeStruct((n, d), F32),
        scratch_shapes=[pltpu.VMEM((tm, d), BF16),
                        pltpu.VMEM((tm, d), F32),
                        pltpu.VMEM((tm + SUBLANES, fc), F32),
                        pltpu.VMEM((nc, SUBLANES, fc), F32)],
        compiler_params=_cparams("arbitrary", "arbitrary"),
        name="conv_ffn",
    )(x, g_in.reshape(1, d), w_up, w_up, w_conv, b_conv.reshape(1, d_ff), w_down,
      g_out.reshape(1, d))


def kernel(x, norm_g, rel_bias, w_in_even, w_out_even, w_sc, w_in_odd, w_out_odd, w_cc, b_cc,
           ln_cc_g, ln_cc_b, w_up, w_ffn_conv, b_ffn_conv, w_down):
    batch, seq, d = x.shape
    depth = norm_g.shape[0]
    sc_width = w_sc.shape[2]
    cc_width = w_cc.shape[2]
    xf = x.reshape(batch * seq, d)
    bias = bias_tables(rel_bias)
    for layer in range(depth):
        i = layer // 2
        if layer % 2 == 0:
            w_in = w_in_even[i].astype(BF16)
            conv_cols = 3 * sc_width
            pa, qkv = norm_matmul(xf, norm_g[layer, 0], w_in,
                                  [(conv_cols, BF16), (w_in.shape[1] - conv_cols, F32)])
            y_a = short_conv(pa, w_sc[i], batch, seq)
            y_b = dilated_attention(qkv, bias, batch, seq)
            w_out = w_out_even[i].astype(BF16)
        else:
            w_in = w_in_odd[i].astype(BF16)
            (p,) = norm_matmul(xf, norm_g[layer, 0], w_in, [(w_in.shape[1], BF16)])
            y_a = conformer_conv(p, w_cc[i], b_cc[i], ln_cc_g[i], ln_cc_b[i], batch, seq)
            y_b = stick_breaking(p, 2 * cc_width, batch, seq)
            w_out = w_out_odd[i].astype(BF16)
        xf = out_proj(y_a, y_b, w_out, norm_g[layer, 1], xf)
        xf = conv_ffn(xf, norm_g[layer, 2], w_up[layer].astype(BF16), w_ffn_conv[layer],
                      b_ffn_conv[layer], w_down[layer].astype(BF16), norm_g[layer, 3], seq)
    return xf.reshape(batch, seq, d)
```

```python
import functools
import math

import numpy as np
import jax
import jax.numpy as jnp
from jax import lax
from jax.experimental import pallas as pl
from jax.experimental.pallas import tpu as pltpu

F32 = jnp.float32
BF16 = jnp.bfloat16

EPS = 1e-6
HEAD_DIM = 64
LANES = 128
SUBLANES = 8
HEADS_PER_TILE = LANES // HEAD_DIM
DA_PAIRS = ((128, 1), (512, 4), (2048, 16))
DA_SPAN = 128
BLOCK = 128
REL_BUCKETS = 32
REL_MAX_DIST = 2048
MASKED = -1e30
SB_EXIT = -105.0
VMEM_LIMIT = 56 * 1024 * 1024


def _cparams(*sem):
    return pltpu.CompilerParams(dimension_semantics=sem, vmem_limit_bytes=VMEM_LIMIT)


def _rms(x, g):
    return x * lax.rsqrt(jnp.mean(x * x, axis=-1, keepdims=True) + EPS) * g


def _norm_matmul_kernel(x_ref, g_ref, w_ref, *o_refs, col_chunk):
    h = _rms(x_ref[...], g_ref[...]).astype(BF16)
    col = 0
    for o_ref in o_refs:
        width = o_ref.shape[1]
        for c in range(0, width, col_chunk):
            o_ref[:, c:c + col_chunk] = jnp.dot(
                h, w_ref[:, col + c:col + c + col_chunk],
                preferred_element_type=F32).astype(o_ref.dtype)
        col += width


def norm_matmul(x, g, w, outs, tm=512, col_chunk=512):
    n, d = x.shape
    assert n % tm == 0 and sum(wd for wd, _ in outs) == w.shape[1]
    assert all(wd % col_chunk == 0 for wd, _ in outs)
    return pl.pallas_call(
        functools.partial(_norm_matmul_kernel, col_chunk=col_chunk),
        grid=(n // tm,),
        in_specs=[pl.BlockSpec((tm, d), lambda i: (i, 0)),
                  pl.BlockSpec((1, d), lambda i: (0, 0)),
                  pl.BlockSpec(w.shape, lambda i: (0, 0))],
        out_specs=[pl.BlockSpec((tm, wd), lambda i: (i, 0)) for wd, _ in outs],
        out_shape=[jax.ShapeDtypeStruct((n, wd), dt) for wd, dt in outs],
        compiler_params=_cparams("parallel"),
        name="norm_in_proj",
    )(x, g.reshape(1, d), w)


def _short_conv_kernel(gb_ref, gc_ref, xa_ref, w_ref, o_ref, pad_ref):
    ts = o_ref.shape[0]
    @pl.when(pl.program_id(1) == 0)
    def _():
        pad_ref[0:SUBLANES, :] = jnp.zeros((SUBLANES, pad_ref.shape[1]), F32)

    @pl.when(pl.program_id(1) != 0)
    def _():
        pad_ref[0:SUBLANES, :] = pad_ref[ts:ts + SUBLANES, :]

    c = gc_ref[...].astype(F32) * xa_ref[...].astype(F32)
    pad_ref[SUBLANES:, :] = c
    conv = (w_ref[2:3, :] * c
            + w_ref[1:2, :] * pad_ref[SUBLANES - 1:SUBLANES - 1 + ts, :]
            + w_ref[0:1, :] * pad_ref[SUBLANES - 2:SUBLANES - 2 + ts, :])
    o_ref[...] = (gb_ref[...].astype(F32) * conv).astype(o_ref.dtype)


def short_conv(pa, w_sc, batch, seq, ts=512):
    n = pa.shape[0]
    width = w_sc.shape[1]
    nt = seq // ts
    return pl.pallas_call(
        _short_conv_kernel,
        grid=(batch, nt),
        in_specs=[pl.BlockSpec((ts, width), lambda b, t: (b * nt + t, 0)),
                  pl.BlockSpec((ts, width), lambda b, t: (b * nt + t, 1)),
                  pl.BlockSpec((ts, width), lambda b, t: (b * nt + t, 2)),
                  pl.BlockSpec(w_sc.shape, lambda b, t: (0, 0))],
        out_specs=pl.BlockSpec((ts, width), lambda b, t: (b * nt + t, 0)),
        out_shape=jax.ShapeDtypeStruct((n, width), BF16),
        scratch_shapes=[pltpu.VMEM((ts + SUBLANES, width), F32)],
        compiler_params=_cparams("arbitrary", "arbitrary"),
        name="short_conv",
    )(pa, pa, pa, w_sc)


def _bucket_tables():
    rel = np.arange(BLOCK)[:, None] - np.arange(2 * BLOCK)[None, :] + DA_SPAN
    valid = (rel >= 0) & (rel <= DA_SPAN)
    max_exact = REL_BUCKETS // 2
    tabs = []
    for _, dil in DA_PAIRS:
        dist = np.clip(rel, 0, DA_SPAN) * dil
        d = np.maximum(dist, 1).astype(np.float32)
        large = max_exact + (np.log(d / np.float32(max_exact))
                             / np.float32(math.log(REL_MAX_DIST / max_exact))
                             * np.float32(REL_BUCKETS - max_exact)).astype(np.int32)
        large = np.minimum(large, REL_BUCKETS - 1)
        bucket = np.where(dist < max_exact, dist, large)
        tabs.append(np.where(valid, bucket, -1))
    return np.stack(tabs).astype(np.int32)


def _bias_table_kernel(rb_ref, idx_ref, o_ref):
    h = pl.program_id(1)
    idx = idx_ref[0]
    acc = jnp.full(idx.shape, MASKED, F32)
    for b in range(REL_BUCKETS):
        acc = jnp.where(idx == b, rb_ref[b, h], acc)
    o_ref[0, 0] = acc


def bias_tables(rel_bias):
    idx = jnp.asarray(_bucket_tables())
    heads = rel_bias.shape[1]
    return pl.pallas_call(
        _bias_table_kernel,
        grid=(len(DA_PAIRS), heads),
        in_specs=[pl.BlockSpec(memory_space=pltpu.SMEM),
                  pl.BlockSpec((1, BLOCK, 2 * BLOCK), lambda g, h: (g, 0, 0))],
        out_specs=pl.BlockSpec((1, 1, BLOCK, 2 * BLOCK), lambda g, h: (g, h, 0, 0)),
        out_shape=jax.ShapeDtypeStruct((len(DA_PAIRS), heads, BLOCK, 2 * BLOCK), F32),
        compiler_params=_cparams("arbitrary", "arbitrary"),
        name="bias_tables",
    )(rel_bias, idx)


def _dilated_attn_kernel(q_ref, k_ref, v_ref, bias_ref, o_ref, m_ref, l_ref, u_ref, *, unroll):
    seq = q_ref.shape[0]
    lane = lax.broadcasted_iota(jnp.int32, (1, LANES), 1)
    head_mask = (lane < HEAD_DIM, lane >= HEAD_DIM)
    prev_cols = lax.broadcasted_iota(jnp.int32, (1, 2 * BLOCK), 1) < BLOCK
    nt_dims = (((1,), (1,)), ((), ()))

    for g, (_, dil) in enumerate(DA_PAIRS):
        nb = seq // dil // BLOCK

        def rows(start):
            if dil == 1:
                return pl.ds(pl.multiple_of(start, BLOCK), BLOCK)
            return pl.ds(start, BLOCK, stride=dil)

        def body(blk, carry):
            r = blk // nb
            j = blk % nb
            cur = rows(r + j * (BLOCK * dil))
            prev = rows(r + jnp.maximum(j - 1, 0) * (BLOCK * dil))
            pen = jnp.where(jnp.logical_and(prev_cols, j == 0), MASKED, 0.0).astype(F32)
            q = q_ref[cur, :] * (1.0 / math.sqrt(HEAD_DIM))
            k2 = jnp.concatenate([k_ref[prev, :], k_ref[cur, :]], axis=0).astype(BF16)
            v2 = jnp.concatenate([v_ref[prev, :], v_ref[cur, :]], axis=0).astype(BF16)
            stats = []
            for h in range(HEADS_PER_TILE):
                qh = jnp.where(head_mask[h], q, 0.0).astype(BF16)
                s = lax.dot_general(qh, k2, nt_dims, preferred_element_type=F32)
                s = s + bias_ref[g, h] + pen
                m = jnp.max(s, axis=-1, keepdims=True)
                p = jnp.exp(s - m)
                l = jnp.sum(p, axis=-1, keepdims=True)
                pv = jnp.dot(p.astype(BF16), v2, preferred_element_type=F32)
                stats.append((jnp.broadcast_to(m, pv.shape), jnp.broadcast_to(l, pv.shape), pv))
            both = [jnp.where(head_mask[0], a, b) for a, b in zip(*stats)]
            m_ref[g, cur, :] = both[0]
            l_ref[g, cur, :] = both[1]
            u_ref[g, cur, :] = both[2]
            return carry

        lax.fori_loop(0, seq // BLOCK, body, 0, unroll=unroll)

    def merge(c, carry):
        rs = pl.ds(pl.multiple_of(c * BLOCK, BLOCK), BLOCK)
        m = m_ref[:, rs, :]
        wgt = jnp.exp(m - jnp.max(m, axis=0, keepdims=True))
        num = jnp.sum(wgt * u_ref[:, rs, :], axis=0)
        den = jnp.sum(wgt * l_ref[:, rs, :], axis=0)
        o_ref[rs, :] = (num / den).astype(o_ref.dtype)
        return carry

    lax.fori_loop(0, seq // BLOCK, merge, 0)


def dilated_attention(qkv, bias, batch, seq, unroll=4):
    n = qkv.shape[0]
    width = qkv.shape[1] // 3
    nhp = width // LANES
    blk = lambda off: pl.BlockSpec((seq, LANES), lambda b, p: (b, off + p))
    return pl.pallas_call(
        functools.partial(_dilated_attn_kernel, unroll=unroll),
        grid=(batch, nhp),
        in_specs=[blk(0), blk(nhp), blk(2 * nhp),
                  pl.BlockSpec((len(DA_PAIRS), HEADS_PER_TILE, BLOCK, 2 * BLOCK),
                               lambda b, p: (0, p, 0, 0))],
        out_specs=pl.BlockSpec((seq, LANES), lambda b, p: (b, p)),
        out_shape=jax.ShapeDtypeStruct((n, width), BF16),
        scratch_shapes=[pltpu.VMEM((len(DA_PAIRS), seq, LANES), F32)] * 3,
        compiler_params=_cparams("parallel", "parallel"),
        name="dilated_attention",
    )(qkv, qkv, qkv, bias)


def _conformer_conv_kernel(a_ref, gate_ref, w_ref, b_ref, lg_ref, lb_ref, o_ref, pad_ref,
                           *, taps, halo, chunk):
    ts = o_ref.shape[0]

    @pl.when(pl.program_id(1) == 0)
    def _():
        pad_ref[0:halo, :] = jnp.zeros((halo, pad_ref.shape[1]), F32)

    @pl.when(pl.program_id(1) != 0)
    def _():
        pad_ref[0:halo, :] = pad_ref[ts:ts + halo, :]

    pad_ref[halo:, :] = a_ref[...].astype(F32) * jax.nn.sigmoid(gate_ref[...].astype(F32))

    lead = halo - (taps - 1)
    groups = -(-taps // SUBLANES)
    for c0 in range(0, ts, chunk):
        acc = jnp.zeros((chunk, o_ref.shape[1]), F32)
        for b in range(SUBLANES):
            n_a = len(range(b, taps, SUBLANES))
            win = pad_ref[c0 + lead + b:c0 + lead + b + chunk + SUBLANES * (n_a - 1), :]
            for a in range(n_a):
                j = SUBLANES * a + b
                acc = acc + w_ref[j:j + 1, :] * win[SUBLANES * a:SUBLANES * a + chunk, :]
        y = acc + b_ref[...]
        mu = jnp.mean(y, axis=-1, keepdims=True)
        yc = y - mu
        var = jnp.mean(yc * yc, axis=-1, keepdims=True)
        z = yc * lax.rsqrt(var + EPS) * lg_ref[...] + lb_ref[...]
        o_ref[c0:c0 + chunk, :] = (z * jax.nn.sigmoid(z)).astype(o_ref.dtype)
    del groups


def conformer_conv(p, w_cc, b_cc, ln_g, ln_b, batch, seq, ts=256, chunk=32):
    n = p.shape[0]
    taps, width = w_cc.shape
    halo = -(-(taps - 1) // SUBLANES) * SUBLANES
    nt = seq // ts
    vec = lambda v: v.reshape(1, width)
    vspec = pl.BlockSpec((1, width), lambda b, t: (0, 0))
    return pl.pallas_call(
        functools.partial(_conformer_conv_kernel, taps=taps, halo=halo, chunk=chunk),
        grid=(batch, nt),
        in_specs=[pl.BlockSpec((ts, width), lambda b, t: (b * nt + t, 0)),
                  pl.BlockSpec((ts, width), lambda b, t: (b * nt + t, 1)),
                  pl.BlockSpec(w_cc.shape, lambda b, t: (0, 0)),
                  vspec, vspec, vspec],
        out_specs=pl.BlockSpec((ts, width), lambda b, t: (b * nt + t, 0)),
        out_shape=jax.ShapeDtypeStruct((n, width), BF16),
        scratch_shapes=[pltpu.VMEM((ts + halo, width), F32)],
        compiler_params=_cparams("arbitrary", "arbitrary"),
        name="conformer_conv",
    )(p, p, w_cc, vec(b_cc), vec(ln_g), vec(ln_b))


def _suffix_sum_matrix():
    j = np.arange(BLOCK)[:, None]
    s = np.arange(BLOCK)[None, :]
    half = np.concatenate([(j > s).astype(np.float32), np.ones((BLOCK, BLOCK), np.float32)], axis=1)
    return np.concatenate([half, half], axis=0)


def _stick_breaking_kernel(q_ref, k_ref, v_ref, u_ref, o_ref, qm_ref, run_ref, acc_ref):
    tq = q_ref.shape[0]
    nsub = tq // BLOCK
    qi = pl.program_id(2)
    lane = lax.broadcasted_iota(jnp.int32, (1, LANES), 1)
    head_mask = (lane < HEAD_DIM, lane >= HEAD_DIM)
    nt_dims = (((1,), (1,)), ((), ()))

    q = q_ref[...].astype(F32) * (1.0 / math.sqrt(HEAD_DIM))
    for h in range(HEADS_PER_TILE):
        qm_ref[h] = jnp.where(head_mask[h], q, 0.0).astype(BF16)
    run_ref[...] = jnp.zeros(run_ref.shape, F32)
    acc_ref[...] = jnp.zeros(acc_ref.shape, F32)

    def sweep(kb, r0, mask):
        ks = pl.ds(pl.multiple_of(kb * BLOCK, BLOCK), BLOCK)
        k_blk = k_ref[ks, :]
        v_blk = v_ref[ks, :]
        for h in range(HEADS_PER_TILE):
            z = lax.dot_general(qm_ref[h, r0:tq, :], k_blk, nt_dims,
                                preferred_element_type=F32)
            soft = jnp.log(1.0 + jnp.exp(-jnp.abs(z)))
            log_beta = jnp.minimum(z, 0.0) - soft
            log_rest = log_beta - z
            if mask is not None:
                log_rest = jnp.where(mask, log_rest, 0.0)
            hi = log_rest.astype(BF16)
            lo = (log_rest - hi.astype(F32)).astype(BF16)
            sums = jnp.dot(jnp.concatenate([hi, lo], axis=1), u_ref[...],
                           preferred_element_type=F32)
            run = run_ref[h, r0:tq, :]
            a = jnp.exp(log_beta + sums[:, 0:BLOCK] + run)
            if mask is not None:
                a = jnp.where(mask, a, 0.0)
            acc_ref[h, r0:tq, :] += jnp.dot(a.astype(BF16), v_blk, preferred_element_type=F32)
            run_ref[h, r0:tq, :] = run + sums[:, BLOCK:2 * BLOCK]

    for j in reversed(range(nsub)):
        r0 = j * BLOCK
        row = lax.broadcasted_iota(jnp.int32, (tq - r0, BLOCK), 0)
        col = lax.broadcasted_iota(jnp.int32, (tq - r0, BLOCK), 1)
        sweep(qi * nsub + j, r0, col < row)

    def still_open():
        return jnp.max(run_ref[...]) > SB_EXIT

    def body(c):
        kb, _ = c
        sweep(kb, 0, None)
        return kb - 1, still_open()

    lax.while_loop(lambda c: jnp.logical_and(c[0] >= 0, c[1]), body,
                   (qi * nsub - 1, still_open()))
    o_ref[...] = jnp.where(head_mask[0], acc_ref[0], acc_ref[1]).astype(o_ref.dtype)


def stick_breaking(p, col0, batch, seq, tq=512):
    n = p.shape[0]
    width = (p.shape[1] - col0) // 3
    nhp = width // LANES
    c0 = col0 // LANES
    nq = seq // tq
    u = jnp.asarray(_suffix_sum_matrix(), BF16)
    return pl.pallas_call(
        _stick_breaking_kernel,
        grid=(batch, nhp, nq),
        in_specs=[pl.BlockSpec((tq, LANES), lambda b, p_, i: (b * nq + i, c0 + p_)),
                  pl.BlockSpec((seq, LANES), lambda b, p_, i: (b, c0 + nhp + p_)),
                  pl.BlockSpec((seq, LANES), lambda b, p_, i: (b, c0 + 2 * nhp + p_)),
                  pl.BlockSpec(u.shape, lambda b, p_, i: (0, 0))],
        out_specs=pl.BlockSpec((tq, LANES), lambda b, p_, i: (b * nq + i, p_)),
        out_shape=jax.ShapeDtypeStruct((n, width), BF16),
        scratch_shapes=[pltpu.VMEM((HEADS_PER_TILE, tq, LANES), BF16),
                        pltpu.VMEM((HEADS_PER_TILE, tq, LANES), F32),
                        pltpu.VMEM((HEADS_PER_TILE, tq, LANES), F32)],
        compiler_params=_cparams("parallel", "parallel", "arbitrary"),
        name="stick_breaking",
    )(p, p, p, u)


def _out_proj_kernel(ya_ref, yb_ref, wa_ref, wb_ref, g_ref, x_ref, o_ref):
    mix = (jnp.dot(ya_ref[...], wa_ref[...], preferred_element_type=F32)
           + jnp.dot(yb_ref[...], wb_ref[...], preferred_element_type=F32))
    o_ref[...] = x_ref[...] + _rms(mix, g_ref[...])


def out_proj(ya, yb, w_out, g, x, tm=512):
    n, d = x.shape
    wa_rows = ya.shape[1]
    wb_rows = yb.shape[1]
    return pl.pallas_call(
        _out_proj_kernel,
        grid=(n // tm,),
        in_specs=[pl.BlockSpec((tm, wa_rows), lambda i: (i, 0)),
                  pl.BlockSpec((tm, wb_rows), lambda i: (i, 0)),
                  pl.BlockSpec((wa_rows, d), lambda i: (0, 0)),
                  pl.BlockSpec((wb_rows, d), lambda i: (wa_rows // wb_rows, 0)),
                  pl.BlockSpec((1, d), lambda i: (0, 0)),
                  pl.BlockSpec((tm, d), lambda i: (i, 0))],
        out_specs=pl.BlockSpec((tm, d), lambda i: (i, 0)),
        out_shape=jax.ShapeDtypeStruct((n, d), F32),
        compiler_params=_cparams("parallel"),
        name="out_proj",
    )(ya, yb, w_out, w_out, g.reshape(1, d), x)


def _conv_ffn_kernel(x_ref, gin_ref, wup_ref, wc_ref, bc_ref, wd_ref, gout_ref, o_ref,
                     act_ref, pad_ref, carry_ref, *, tiles_per_seq, fc):
    tm = x_ref.shape[0]
    d_ff = wd_ref.shape[0]
    x = x_ref[...]
    h = _rms(x, gin_ref[...]).astype(BF16)

    @pl.when(pl.program_id(0) % tiles_per_seq == 0)
    def _():
        carry_ref[...] = jnp.zeros(carry_ref.shape, F32)

    for n, c0 in enumerate(range(0, d_ff, fc)):
        cols = slice(c0, c0 + fc)
        gate = jnp.dot(h, wup_ref[:, cols], preferred_element_type=F32)
        up = jnp.dot(h, wup_ref[:, d_ff + c0:d_ff + c0 + fc], preferred_element_type=F32)
        pad = pad_ref.at[n % pad_ref.shape[0]]
        pad[0:SUBLANES, :] = carry_ref[:, cols]
        pad[SUBLANES:, :] = gate
        carry_ref[:, cols] = gate[tm - SUBLANES:, :]
        conv = (wc_ref[2:3, cols] * gate
                + wc_ref[1:2, cols] * pad[SUBLANES - 1:SUBLANES - 1 + tm, :]
                + wc_ref[0:1, cols] * pad[SUBLANES - 2:SUBLANES - 2 + tm, :]
                + bc_ref[:, cols])
        act_ref[:, cols] = (conv * jax.nn.sigmoid(conv) * up).astype(BF16)
    out = jnp.dot(act_ref[...], wd_ref[...], preferred_element_type=F32)
    o_ref[...] = x + _rms(out, gout_ref[...])


def conv_ffn(x, g_in, w_up, w_conv, b_conv, w_down, g_out, seq, tm=512, fc=256):
    n, d = x.shape
    d_ff = w_down.shape[0]
    assert d_ff % fc == 0 and seq % tm == 0
    const = lambda shape: pl.BlockSpec(shape, lambda i: (0, 0), pipeline_mode=pl.Buffered(1))
    return pl.pallas_call(
        functools.partial(_conv_ffn_kernel, tiles_per_seq=seq // tm, fc=fc),
        grid=(n // tm,),
        in_specs=[pl.BlockSpec((tm, d), lambda i: (i, 0)),
                  const((1, d)),
                  const(w_up.shape),
                  const(w_conv.shape),
                  const((1, d_ff)),
                  const(w_down.shape),
                  const((1, d))],
        out_specs=pl.BlockSpec((tm, d), lambda i: (i, 0)),
        out_shape=jax.ShapeDtypeStruct((n, d), F32),
        scratch_shapes=[pltpu.VMEM((tm, d_ff), BF16),
                        pltpu.VMEM((2, tm + SUBLANES, fc), F32),
                        pltpu.VMEM((SUBLANES, d_ff), F32)],
        compiler_params=_cparams("arbitrary"),
        name="conv_ffn",
    )(x, g_in.reshape(1, d), w_up, w_conv, b_conv.reshape(1, d_ff), w_down,
      g_out.reshape(1, d))


def kernel(x, norm_g, rel_bias, w_in_even, w_out_even, w_sc, w_in_odd, w_out_odd, w_cc, b_cc,
           ln_cc_g, ln_cc_b, w_up, w_ffn_conv, b_ffn_conv, w_down):
    batch, seq, d = x.shape
    depth = norm_g.shape[0]
    sc_width = w_sc.shape[2]
    cc_width = w_cc.shape[2]
    xf = x.reshape(batch * seq, d)
    bias = bias_tables(rel_bias)
    for layer in range(depth):
        i = layer // 2
        if layer % 2 == 0:
            w_in = w_in_even[i].astype(BF16)
            conv_cols = 3 * sc_width
            pa, qkv = norm_matmul(xf, norm_g[layer, 0], w_in,
                                  [(conv_cols, BF16), (w_in.shape[1] - conv_cols, F32)])
            y_a = short_conv(pa, w_sc[i], batch, seq)
            y_b = dilated_attention(qkv, bias, batch, seq)
            w_out = w_out_even[i].astype(BF16)
        else:
            w_in = w_in_odd[i].astype(BF16)
            (p,) = norm_matmul(xf, norm_g[layer, 0], w_in, [(w_in.shape[1], BF16)])
            y_a = conformer_conv(p, w_cc[i], b_cc[i], ln_cc_g[i], ln_cc_b[i], batch, seq)
            y_b = stick_breaking(p, 2 * cc_width, batch, seq)
            w_out = w_out_odd[i].astype(BF16)
        xf = out_proj(y_a, y_b, w_out, norm_g[layer, 1], xf)
        xf = conv_ffn(xf, norm_g[layer, 2], w_up[layer].astype(BF16), w_ffn_conv[layer],
                      b_ffn_conv[layer], w_down[layer].astype(BF16), norm_g[layer, 3], seq)
    return xf.reshape(batch, seq, d)
```

```python
import functools
import math

import numpy as np
import jax
import jax.numpy as jnp
from jax import lax
from jax.experimental import pallas as pl
from jax.experimental.pallas import tpu as pltpu

F32 = jnp.float32
BF16 = jnp.bfloat16

EPS = 1e-6
HEAD_DIM = 64
LANES = 128
SUBLANES = 8
HEADS_PER_TILE = LANES // HEAD_DIM
DA_PAIRS = ((128, 1), (512, 4), (2048, 16))
DA_SPAN = 128
BLOCK = 128
REL_BUCKETS = 32
REL_MAX_DIST = 2048
MASKED = -1e30
SB_EXIT = -105.0
VMEM_LIMIT = 56 * 1024 * 1024


def _cparams(*sem):
    return pltpu.CompilerParams(dimension_semantics=sem, vmem_limit_bytes=VMEM_LIMIT)


def _rms(x, g):
    return x * lax.rsqrt(jnp.mean(x * x, axis=-1, keepdims=True) + EPS) * g


def _norm_matmul_kernel(x_ref, g_ref, w_ref, *o_refs, col_chunk):
    h = _rms(x_ref[...], g_ref[...]).astype(BF16)
    col = 0
    for o_ref in o_refs:
        width = o_ref.shape[1]
        for c in range(0, width, col_chunk):
            o_ref[:, c:c + col_chunk] = jnp.dot(
                h, w_ref[:, col + c:col + c + col_chunk],
                preferred_element_type=F32).astype(o_ref.dtype)
        col += width


def norm_matmul(x, g, w, outs, tm=512, col_chunk=512):
    n, d = x.shape
    assert n % tm == 0 and sum(wd for wd, _ in outs) == w.shape[1]
    assert all(wd % col_chunk == 0 for wd, _ in outs)
    return pl.pallas_call(
        functools.partial(_norm_matmul_kernel, col_chunk=col_chunk),
        grid=(n // tm,),
        in_specs=[pl.BlockSpec((tm, d), lambda i: (i, 0)),
                  pl.BlockSpec((1, d), lambda i: (0, 0)),
                  pl.BlockSpec(w.shape, lambda i: (0, 0))],
        out_specs=[pl.BlockSpec((tm, wd), lambda i: (i, 0)) for wd, _ in outs],
        out_shape=[jax.ShapeDtypeStruct((n, wd), dt) for wd, dt in outs],
        compiler_params=_cparams("parallel"),
        name="norm_in_proj",
    )(x, g.reshape(1, d), w)


def _short_conv_kernel(gb_ref, gc_ref, xa_ref, w_ref, o_ref, pad_ref):
    ts = o_ref.shape[0]
    @pl.when(pl.program_id(1) == 0)
    def _():
        pad_ref[0:SUBLANES, :] = jnp.zeros((SUBLANES, pad_ref.shape[1]), F32)

    @pl.when(pl.program_id(1) != 0)
    def _():
        pad_ref[0:SUBLANES, :] = pad_ref[ts:ts + SUBLANES, :]

    c = gc_ref[...].astype(F32) * xa_ref[...].astype(F32)
    pad_ref[SUBLANES:, :] = c
    conv = (w_ref[2:3, :] * c
            + w_ref[1:2, :] * pad_ref[SUBLANES - 1:SUBLANES - 1 + ts, :]
            + w_ref[0:1, :] * pad_ref[SUBLANES - 2:SUBLANES - 2 + ts, :])
    o_ref[...] = (gb_ref[...].astype(F32) * conv).astype(o_ref.dtype)


def short_conv(pa, w_sc, batch, seq, ts=512):
    n = pa.shape[0]
    width = w_sc.shape[1]
    nt = seq // ts
    return pl.pallas_call(
        _short_conv_kernel,
        grid=(batch, nt),
        in_specs=[pl.BlockSpec((ts, width), lambda b, t: (b * nt + t, 0)),
                  pl.BlockSpec((ts, width), lambda b, t: (b * nt + t, 1)),
                  pl.BlockSpec((ts, width), lambda b, t: (b * nt + t, 2)),
                  pl.BlockSpec(w_sc.shape, lambda b, t: (0, 0))],
        out_specs=pl.BlockSpec((ts, width), lambda b, t: (b * nt + t, 0)),
        out_shape=jax.ShapeDtypeStruct((n, width), BF16),
        scratch_shapes=[pltpu.VMEM((ts + SUBLANES, width), F32)],
        compiler_params=_cparams("arbitrary", "arbitrary"),
        name="short_conv",
    )(pa, pa, pa, w_sc)


def _bucket_tables():
    rel = np.arange(BLOCK)[:, None] - np.arange(2 * BLOCK)[None, :] + DA_SPAN
    valid = (rel >= 0) & (rel <= DA_SPAN)
    max_exact = REL_BUCKETS // 2
    tabs = []
    for _, dil in DA_PAIRS:
        dist = np.clip(rel, 0, DA_SPAN) * dil
        d = np.maximum(dist, 1).astype(np.float32)
        large = max_exact + (np.log(d / np.float32(max_exact))
                             / np.float32(math.log(REL_MAX_DIST / max_exact))
                             * np.float32(REL_BUCKETS - max_exact)).astype(np.int32)
        large = np.minimum(large, REL_BUCKETS - 1)
        bucket = np.where(dist < max_exact, dist, large)
        tabs.append(np.where(valid, bucket, -1))
        tabs.append(np.where(valid & (np.arange(2 * BLOCK)[None, :] >= BLOCK), bucket, -1))
    return np.stack(tabs).astype(np.int32)


def _bias_table_kernel(rb_ref, idx_ref, o_ref):
    h = pl.program_id(1)
    idx = idx_ref[0]
    acc = jnp.full(idx.shape, MASKED, F32)
    for b in range(REL_BUCKETS):
        acc = jnp.where(idx == b, rb_ref[b, h], acc)
    o_ref[0, 0] = acc


def bias_tables(rel_bias):
    idx = jnp.asarray(_bucket_tables())
    heads = rel_bias.shape[1]
    return pl.pallas_call(
        _bias_table_kernel,
        grid=(idx.shape[0], heads),
        in_specs=[pl.BlockSpec(memory_space=pltpu.SMEM),
                  pl.BlockSpec((1, BLOCK, 2 * BLOCK), lambda g, h: (g, 0, 0))],
        out_specs=pl.BlockSpec((1, 1, BLOCK, 2 * BLOCK), lambda g, h: (g, h, 0, 0)),
        out_shape=jax.ShapeDtypeStruct((idx.shape[0], heads, BLOCK, 2 * BLOCK), F32),
        compiler_params=_cparams("arbitrary", "arbitrary"),
        name="bias_tables",
    )(rel_bias, idx)


def _dilated_attn_kernel(q_ref, k_ref, v_ref, bias_ref, o_ref, m_ref, l_ref, u_ref, *, unroll):
    seq = q_ref.shape[0]
    lane = lax.broadcasted_iota(jnp.int32, (1, LANES), 1)
    head_mask = (lane < HEAD_DIM, lane >= HEAD_DIM)
    nt_dims = (((1,), (1,)), ((), ()))

    for g, (_, dil) in enumerate(DA_PAIRS):
        nb = seq // dil // BLOCK

        def rows(start):
            if dil == 1:
                return pl.ds(pl.multiple_of(start, BLOCK), BLOCK)
            return pl.ds(start, BLOCK, stride=dil)

        def body(blk, carry):
            r = blk // nb
            j = blk % nb
            cur = rows(r + j * (BLOCK * dil))
            prev = rows(r + jnp.maximum(j - 1, 0) * (BLOCK * dil))
            table = 2 * g + (j == 0).astype(jnp.int32)
            q = q_ref[cur, :] * (1.0 / math.sqrt(HEAD_DIM))
            k2 = jnp.concatenate([k_ref[prev, :], k_ref[cur, :]], axis=0).astype(BF16)
            v2 = jnp.concatenate([v_ref[prev, :], v_ref[cur, :]], axis=0).astype(BF16)
            v2 = jnp.concatenate([v2, jnp.ones(v2.shape, BF16)], axis=1)
            stats = []
            for h in range(HEADS_PER_TILE):
                qh = jnp.where(head_mask[h], q, 0.0).astype(BF16)
                s = lax.dot_general(qh, k2, nt_dims, preferred_element_type=F32)
                s = s + bias_ref[table, h]
                m = jnp.max(s, axis=-1, keepdims=True)
                p = jnp.exp(s - m)
                pv = jnp.dot(p.astype(BF16), v2, preferred_element_type=F32)
                stats.append((jnp.broadcast_to(m, (BLOCK, LANES)), pv[:, LANES:], pv[:, :LANES]))
            both = [jnp.where(head_mask[0], a, b) for a, b in zip(*stats)]
            m_ref[g, cur, :] = both[0]
            l_ref[g, cur, :] = both[1]
            u_ref[g, cur, :] = both[2]
            return carry

        lax.fori_loop(0, seq // BLOCK, body, 0, unroll=unroll)

    def merge(c, carry):
        rs = pl.ds(pl.multiple_of(c * BLOCK, BLOCK), BLOCK)
        m = m_ref[:, rs, :]
        wgt = jnp.exp(m - jnp.max(m, axis=0, keepdims=True))
        num = jnp.sum(wgt * u_ref[:, rs, :], axis=0)
        den = jnp.sum(wgt * l_ref[:, rs, :], axis=0)
        o_ref[rs, :] = (num / den).astype(o_ref.dtype)
        return carry

    lax.fori_loop(0, seq // BLOCK, merge, 0)


def dilated_attention(qkv, bias, batch, seq, unroll=4):
    n = qkv.shape[0]
    width = qkv.shape[1] // 3
    nhp = width // LANES
    blk = lambda off: pl.BlockSpec((seq, LANES), lambda b, p: (b, off + p))
    return pl.pallas_call(
        functools.partial(_dilated_attn_kernel, unroll=unroll),
        grid=(batch, nhp),
        in_specs=[blk(0), blk(nhp), blk(2 * nhp),
                  pl.BlockSpec((bias.shape[0], HEADS_PER_TILE, BLOCK, 2 * BLOCK),
                               lambda b, p: (0, p, 0, 0))],
        out_specs=pl.BlockSpec((seq, LANES), lambda b, p: (b, p)),
        out_shape=jax.ShapeDtypeStruct((n, width), BF16),
        scratch_shapes=[pltpu.VMEM((len(DA_PAIRS), seq, LANES), F32)] * 3,
        compiler_params=_cparams("parallel", "parallel"),
        name="dilated_attention",
    )(qkv, qkv, qkv, bias)


def _conformer_conv_kernel(a_ref, gate_ref, w_ref, b_ref, lg_ref, lb_ref, o_ref, pad_ref,
                           *, taps, halo, chunk):
    ts = o_ref.shape[0]

    @pl.when(pl.program_id(1) == 0)
    def _():
        pad_ref[0:halo, :] = jnp.zeros((halo, pad_ref.shape[1]), F32)

    @pl.when(pl.program_id(1) != 0)
    def _():
        pad_ref[0:halo, :] = pad_ref[ts:ts + halo, :]

    pad_ref[halo:, :] = a_ref[...].astype(F32) * jax.nn.sigmoid(gate_ref[...].astype(F32))

    lead = halo - (taps - 1)
    groups = -(-taps // SUBLANES)
    for c0 in range(0, ts, chunk):
        acc = jnp.zeros((chunk, o_ref.shape[1]), F32)
        for b in range(SUBLANES):
            n_a = len(range(b, taps, SUBLANES))
            win = pad_ref[c0 + lead + b:c0 + lead + b + chunk + SUBLANES * (n_a - 1), :]
            for a in range(n_a):
                j = SUBLANES * a + b
                acc = acc + w_ref[j:j + 1, :] * win[SUBLANES * a:SUBLANES * a + chunk, :]
        y = acc + b_ref[...]
        mu = jnp.mean(y, axis=-1, keepdims=True)
        yc = y - mu
        var = jnp.mean(yc * yc, axis=-1, keepdims=True)
        z = yc * lax.rsqrt(var + EPS) * lg_ref[...] + lb_ref[...]
        o_ref[c0:c0 + chunk, :] = (z * jax.nn.sigmoid(z)).astype(o_ref.dtype)
    del groups


def conformer_conv(p, w_cc, b_cc, ln_g, ln_b, batch, seq, ts=256, chunk=32):
    n = p.shape[0]
    taps, width = w_cc.shape
    halo = -(-(taps - 1) // SUBLANES) * SUBLANES
    nt = seq // ts
    vec = lambda v: v.reshape(1, width)
    vspec = pl.BlockSpec((1, width), lambda b, t: (0, 0))
    return pl.pallas_call(
        functools.partial(_conformer_conv_kernel, taps=taps, halo=halo, chunk=chunk),
        grid=(batch, nt),
        in_specs=[pl.BlockSpec((ts, width), lambda b, t: (b * nt + t, 0)),
                  pl.BlockSpec((ts, width), lambda b, t: (b * nt + t, 1)),
                  pl.BlockSpec(w_cc.shape, lambda b, t: (0, 0)),
                  vspec, vspec, vspec],
        out_specs=pl.BlockSpec((ts, width), lambda b, t: (b * nt + t, 0)),
        out_shape=jax.ShapeDtypeStruct((n, width), BF16),
        scratch_shapes=[pltpu.VMEM((ts + halo, width), F32)],
        compiler_params=_cparams("arbitrary", "arbitrary"),
        name="conformer_conv",
    )(p, p, w_cc, vec(b_cc), vec(ln_g), vec(ln_b))


def _suffix_sum_matrix():
    j = np.arange(BLOCK)[:, None]
    s = np.arange(BLOCK)[None, :]
    half = -np.concatenate([(j >= s).astype(np.float32), np.ones((BLOCK, BLOCK), np.float32)],
                           axis=1)
    return np.concatenate([half, half], axis=0)


def _stick_breaking_kernel(q_ref, k_ref, v_ref, u_ref, o_ref, qm_ref, run_ref, acc_ref, cat_ref):
    tq = q_ref.shape[0]
    nsub = tq // BLOCK
    qi = pl.program_id(2)
    lane = lax.broadcasted_iota(jnp.int32, (1, LANES), 1)
    head_mask = (lane < HEAD_DIM, lane >= HEAD_DIM)
    nt_dims = (((1,), (1,)), ((), ()))
    rows2 = HEADS_PER_TILE * BLOCK
    row = lax.broadcasted_iota(jnp.int32, (rows2, BLOCK), 0)
    col = lax.broadcasted_iota(jnp.int32, (rows2, BLOCK), 1)
    before = col < jnp.bitwise_and(row, BLOCK - 1)

    for r in range(nsub):
        q = q_ref[r * BLOCK:(r + 1) * BLOCK, :].astype(F32) * (1.0 / math.sqrt(HEAD_DIM))
        qm_ref[r] = jnp.concatenate(
            [jnp.where(head_mask[h], q, 0.0) for h in range(HEADS_PER_TILE)], axis=0).astype(BF16)
    run_ref[...] = jnp.zeros(run_ref.shape, F32)
    acc_ref[...] = jnp.zeros(acc_ref.shape, F32)

    def step(d, own_block):
        kbs = [qi * nsub + r - d for r in range(nsub)]
        kss = [pl.ds(pl.multiple_of(jnp.maximum(kb, 0) * BLOCK, BLOCK), BLOCK) for kb in kbs]
        zs = [lax.dot_general(qm_ref[r], k_ref[kss[r], :], nt_dims, preferred_element_type=F32)
              for r in range(nsub)]
        for r, z in enumerate(zs):
            neg_log_rest = jnp.maximum(z, 0.0) + jnp.log(1.0 + jnp.exp(-jnp.abs(z)))
            if own_block:
                neg_log_rest = jnp.where(before, neg_log_rest, 0.0)
            hi = neg_log_rest.astype(BF16)
            lo = (neg_log_rest - hi.astype(F32)).astype(BF16)
            cat_ref[r * rows2:(r + 1) * rows2, :] = jnp.concatenate([hi, lo], axis=1)
        sums = jnp.dot(cat_ref[...], u_ref[...], preferred_element_type=F32)
        weights = []
        for r in range(nsub):
            sub = sums[r * rows2:(r + 1) * rows2]
            run = run_ref[r]
            if not own_block:
                run = jnp.where(kbs[r] < 0, MASKED, run)
            a = jnp.exp(zs[r] + sub[:, 0:BLOCK] + run)
            if own_block:
                a = jnp.where(before, a, 0.0)
            weights.append(a.astype(BF16))
            run_ref[r] = run + sub[:, BLOCK:2 * BLOCK]
        for r in range(nsub):
            acc_ref[r] += jnp.dot(weights[r], v_ref[kss[r], :], preferred_element_type=F32)

    def still_open():
        return jnp.max(run_ref[...]) > SB_EXIT

    def body(c):
        d, _ = c
        step(d, False)
        return d + 1, still_open()

    step(0, True)
    lax.while_loop(lambda c: c[1], body, (jnp.int32(1), still_open()))
    for r in range(nsub):
        o_ref[r * BLOCK:(r + 1) * BLOCK, :] = jnp.where(
            head_mask[0], acc_ref[r, 0:BLOCK, :], acc_ref[r, BLOCK:rows2, :]).astype(o_ref.dtype)


def stick_breaking(p, col0, batch, seq, tq=1024):
    n = p.shape[0]
    width = (p.shape[1] - col0) // 3
    nhp = width // LANES
    c0 = col0 // LANES
    nq = seq // tq
    u = jnp.asarray(_suffix_sum_matrix(), BF16)
    return pl.pallas_call(
        _stick_breaking_kernel,
        grid=(batch, nhp, nq),
        in_specs=[pl.BlockSpec((tq, LANES), lambda b, p_, i: (b * nq + i, c0 + p_)),
                  pl.BlockSpec((seq, LANES), lambda b, p_, i: (b, c0 + nhp + p_)),
                  pl.BlockSpec((seq, LANES), lambda b, p_, i: (b, c0 + 2 * nhp + p_)),
                  pl.BlockSpec(u.shape, lambda b, p_, i: (0, 0))],
        out_specs=pl.BlockSpec((tq, LANES), lambda b, p_, i: (b * nq + i, p_)),
        out_shape=jax.ShapeDtypeStruct((n, width), BF16),
        scratch_shapes=[pltpu.VMEM((tq // BLOCK, HEADS_PER_TILE * BLOCK, LANES), BF16),
                        pltpu.VMEM((tq // BLOCK, HEADS_PER_TILE * BLOCK, LANES), F32),
                        pltpu.VMEM((tq // BLOCK, HEADS_PER_TILE * BLOCK, LANES), F32),
                        pltpu.VMEM((HEADS_PER_TILE * tq, 2 * BLOCK), BF16)],
        compiler_params=_cparams("parallel", "parallel", "arbitrary"),
        name="stick_breaking",
    )(p, p, p, u)


def _out_proj_kernel(ya_ref, yb_ref, wa_ref, wb_ref, g_ref, x_ref, o_ref):
    mix = (jnp.dot(ya_ref[...], wa_ref[...], preferred_element_type=F32)
           + jnp.dot(yb_ref[...], wb_ref[...], preferred_element_type=F32))
    o_ref[...] = x_ref[...] + _rms(mix, g_ref[...])


def out_proj(ya, yb, w_out, g, x, tm=512):
    n, d = x.shape
    wa_rows = ya.shape[1]
    wb_rows = yb.shape[1]
    return pl.pallas_call(
        _out_proj_kernel,
        grid=(n // tm,),
        in_specs=[pl.BlockSpec((tm, wa_rows), lambda i: (i, 0)),
                  pl.BlockSpec((tm, wb_rows), lambda i: (i, 0)),
                  pl.BlockSpec((wa_rows, d), lambda i: (0, 0)),
                  pl.BlockSpec((wb_rows, d), lambda i: (wa_rows // wb_rows, 0)),
                  pl.BlockSpec((1, d), lambda i: (0, 0)),
                  pl.BlockSpec((tm, d), lambda i: (i, 0))],
        out_specs=pl.BlockSpec((tm, d), lambda i: (i, 0)),
        out_shape=jax.ShapeDtypeStruct((n, d), F32),
        compiler_params=_cparams("parallel"),
        name="out_proj",
    )(ya, yb, w_out, w_out, g.reshape(1, d), x)


def _conv_ffn_kernel(x_ref, gin_ref, wup_ref, wc_ref, bc_ref, wd_ref, gout_ref, o_ref,
                     act_ref, pad_ref, carry_ref, *, tiles_per_seq, fc):
    tm = x_ref.shape[0]
    d_ff = wd_ref.shape[0]
    x = x_ref[...]
    h = _rms(x, gin_ref[...]).astype(BF16)

    @pl.when(pl.program_id(0) % tiles_per_seq == 0)
    def _():
        carry_ref[...] = jnp.zeros(carry_ref.shape, F32)

    for n, c0 in enumerate(range(0, d_ff, fc)):
        cols = slice(c0, c0 + fc)
        gate = jnp.dot(h, wup_ref[:, cols], preferred_element_type=F32)
        up = jnp.dot(h, wup_ref[:, d_ff + c0:d_ff + c0 + fc], preferred_element_type=F32)
        pad = pad_ref.at[n % pad_ref.shape[0]]
        pad[0:SUBLANES, :] = carry_ref[:, cols]
        pad[SUBLANES:, :] = gate
        carry_ref[:, cols] = gate[tm - SUBLANES:, :]
        conv = (wc_ref[2:3, cols] * gate
                + wc_ref[1:2, cols] * pad[SUBLANES - 1:SUBLANES - 1 + tm, :]
                + wc_ref[0:1, cols] * pad[SUBLANES - 2:SUBLANES - 2 + tm, :]
                + bc_ref[:, cols])
        act_ref[:, cols] = (conv * jax.nn.sigmoid(conv) * up).astype(BF16)
    out = jnp.dot(act_ref[...], wd_ref[...], preferred_element_type=F32)
    o_ref[...] = x + _rms(out, gout_ref[...])


def conv_ffn(x, g_in, w_up, w_conv, b_conv, w_down, g_out, seq, tm=512, fc=256):
    n, d = x.shape
    d_ff = w_down.shape[0]
    assert d_ff % fc == 0 and seq % tm == 0
    const = lambda shape: pl.BlockSpec(shape, lambda i: (0, 0), pipeline_mode=pl.Buffered(1))
    return pl.pallas_call(
        functools.partial(_conv_ffn_kernel, tiles_per_seq=seq // tm, fc=fc),
        grid=(n // tm,),
        in_specs=[pl.BlockSpec((tm, d), lambda i: (i, 0)),
                  const((1, d)),
                  const(w_up.shape),
                  const(w_conv.shape),
                  const((1, d_ff)),
                  const(w_down.shape),
                  const((1, d))],
        out_specs=pl.BlockSpec((tm, d), lambda i: (i, 0)),
        out_shape=jax.ShapeDtypeStruct((n, d), F32),
        scratch_shapes=[pltpu.VMEM((tm, d_ff), BF16),
                        pltpu.VMEM((2, tm + SUBLANES, fc), F32),
                        pltpu.VMEM((SUBLANES, d_ff), F32)],
        compiler_params=_cparams("arbitrary"),
        name="conv_ffn",
    )(x, g_in.reshape(1, d), w_up, w_conv, b_conv.reshape(1, d_ff), w_down,
      g_out.reshape(1, d))


def kernel(x, norm_g, rel_bias, w_in_even, w_out_even, w_sc, w_in_odd, w_out_odd, w_cc, b_cc,
           ln_cc_g, ln_cc_b, w_up, w_ffn_conv, b_ffn_conv, w_down):
    batch, seq, d = x.shape
    depth = norm_g.shape[0]
    sc_width = w_sc.shape[2]
    cc_width = w_cc.shape[2]
    xf = x.reshape(batch * seq, d)
    bias = bias_tables(rel_bias)
    for layer in range(depth):
        i = layer // 2
        if layer % 2 == 0:
            w_in = w_in_even[i].astype(BF16)
            conv_cols = 3 * sc_width
            pa, qkv = norm_matmul(xf, norm_g[layer, 0], w_in,
                                  [(conv_cols, BF16), (w_in.shape[1] - conv_cols, F32)])
            y_a = short_conv(pa, w_sc[i], batch, seq)
            y_b = dilated_attention(qkv, bias, batch, seq)
            w_out = w_out_even[i].astype(BF16)
        else:
            w_in = w_in_odd[i].astype(BF16)
            (p,) = norm_matmul(xf, norm_g[layer, 0], w_in, [(w_in.shape[1], BF16)])
            y_a = conformer_conv(p, w_cc[i], b_cc[i], ln_cc_g[i], ln_cc_b[i], batch, seq)
            y_b = stick_breaking(p, 2 * cc_width, batch, seq)
            w_out = w_out_odd[i].astype(BF16)
        xf = out_proj(y_a, y_b, w_out, norm_g[layer, 1], xf)
        xf = conv_ffn(xf, norm_g[layer, 2], w_up[layer].astype(BF16), w_ffn_conv[layer],
                      b_ffn_conv[layer], w_down[layer].astype(BF16), norm_g[layer, 3], seq)
    return xf.reshape(batch, seq, d)
```

```python
import functools
import math

import numpy as np
import jax
import jax.numpy as jnp
from jax import lax
from jax.experimental import pallas as pl
from jax.experimental.pallas import tpu as pltpu

F32 = jnp.float32
BF16 = jnp.bfloat16

EPS = 1e-6
HEAD_DIM = 64
LANES = 128
SUBLANES = 8
HEADS_PER_TILE = LANES // HEAD_DIM
DA_PAIRS = ((128, 1), (512, 4), (2048, 16))
DA_SPAN = 128
BLOCK = 128
REL_BUCKETS = 32
REL_MAX_DIST = 2048
MASKED = -1e30
SB_EXIT = -105.0
VMEM_LIMIT = 56 * 1024 * 1024


def _cparams(*sem):
    return pltpu.CompilerParams(dimension_semantics=sem, vmem_limit_bytes=VMEM_LIMIT)


def _rms(x, g):
    return x * lax.rsqrt(jnp.mean(x * x, axis=-1, keepdims=True) + EPS) * g


def _norm_matmul_kernel(x_ref, g_ref, w_ref, *o_refs, col_chunk):
    h = _rms(x_ref[...], g_ref[...]).astype(BF16)
    col = 0
    for o_ref in o_refs:
        width = o_ref.shape[1]
        for c in range(0, width, col_chunk):
            o_ref[:, c:c + col_chunk] = jnp.dot(
                h, w_ref[:, col + c:col + c + col_chunk],
                preferred_element_type=F32).astype(o_ref.dtype)
        col += width


def norm_matmul(x, g, w, outs, tm=512, col_chunk=512):
    n, d = x.shape
    assert n % tm == 0 and sum(wd for wd, _ in outs) == w.shape[1]
    assert all(wd % col_chunk == 0 for wd, _ in outs)
    return pl.pallas_call(
        functools.partial(_norm_matmul_kernel, col_chunk=col_chunk),
        grid=(n // tm,),
        in_specs=[pl.BlockSpec((tm, d), lambda i: (i, 0)),
                  pl.BlockSpec((1, d), lambda i: (0, 0)),
                  pl.BlockSpec(w.shape, lambda i: (0, 0))],
        out_specs=[pl.BlockSpec((tm, wd), lambda i: (i, 0)) for wd, _ in outs],
        out_shape=[jax.ShapeDtypeStruct((n, wd), dt) for wd, dt in outs],
        compiler_params=_cparams("parallel"),
        name="norm_in_proj",
    )(x, g.reshape(1, d), w)


def _short_conv_kernel(gb_ref, gc_ref, xa_ref, w_ref, o_ref, pad_ref):
    ts = o_ref.shape[0]
    @pl.when(pl.program_id(1) == 0)
    def _():
        pad_ref[0:SUBLANES, :] = jnp.zeros((SUBLANES, pad_ref.shape[1]), F32)

    @pl.when(pl.program_id(1) != 0)
    def _():
        pad_ref[0:SUBLANES, :] = pad_ref[ts:ts + SUBLANES, :]

    c = gc_ref[...].astype(F32) * xa_ref[...].astype(F32)
    pad_ref[SUBLANES:, :] = c
    conv = (w_ref[2:3, :] * c
            + w_ref[1:2, :] * pad_ref[SUBLANES - 1:SUBLANES - 1 + ts, :]
            + w_ref[0:1, :] * pad_ref[SUBLANES - 2:SUBLANES - 2 + ts, :])
    o_ref[...] = (gb_ref[...].astype(F32) * conv).astype(o_ref.dtype)


def short_conv(pa, w_sc, batch, seq, ts=512):
    n = pa.shape[0]
    width = w_sc.shape[1]
    nt = seq // ts
    return pl.pallas_call(
        _short_conv_kernel,
        grid=(batch, nt),
        in_specs=[pl.BlockSpec((ts, width), lambda b, t: (b * nt + t, 0)),
                  pl.BlockSpec((ts, width), lambda b, t: (b * nt + t, 1)),
                  pl.BlockSpec((ts, width), lambda b, t: (b * nt + t, 2)),
                  pl.BlockSpec(w_sc.shape, lambda b, t: (0, 0))],
        out_specs=pl.BlockSpec((ts, width), lambda b, t: (b * nt + t, 0)),
        out_shape=jax.ShapeDtypeStruct((n, width), BF16),
        scratch_shapes=[pltpu.VMEM((ts + SUBLANES, width), F32)],
        compiler_params=_cparams("arbitrary", "arbitrary"),
        name="short_conv",
    )(pa, pa, pa, w_sc)


def _bucket_tables():
    rel = np.arange(BLOCK)[:, None] - np.arange(2 * BLOCK)[None, :] + DA_SPAN
    valid = (rel >= 0) & (rel <= DA_SPAN)
    max_exact = REL_BUCKETS // 2
    tabs = []
    for _, dil in DA_PAIRS:
        dist = np.clip(rel, 0, DA_SPAN) * dil
        d = np.maximum(dist, 1).astype(np.float32)
        large = max_exact + (np.log(d / np.float32(max_exact))
                             / np.float32(math.log(REL_MAX_DIST / max_exact))
                             * np.float32(REL_BUCKETS - max_exact)).astype(np.int32)
        large = np.minimum(large, REL_BUCKETS - 1)
        bucket = np.where(dist < max_exact, dist, large)
        tabs.append(np.where(valid, bucket, -1))
        tabs.append(np.where(valid & (np.arange(2 * BLOCK)[None, :] >= BLOCK), bucket, -1))
    return np.stack(tabs).astype(np.int32)


def _bias_table_kernel(rb_ref, idx_ref, o_ref):
    h = pl.program_id(1)
    idx = idx_ref[0]
    acc = jnp.full(idx.shape, MASKED, F32)
    for b in range(REL_BUCKETS):
        acc = jnp.where(idx == b, rb_ref[b, h], acc)
    o_ref[0, 0] = acc


def bias_tables(rel_bias):
    idx = jnp.asarray(_bucket_tables())
    heads = rel_bias.shape[1]
    return pl.pallas_call(
        _bias_table_kernel,
        grid=(idx.shape[0], heads),
        in_specs=[pl.BlockSpec(memory_space=pltpu.SMEM),
                  pl.BlockSpec((1, BLOCK, 2 * BLOCK), lambda g, h: (g, 0, 0))],
        out_specs=pl.BlockSpec((1, 1, BLOCK, 2 * BLOCK), lambda g, h: (g, h, 0, 0)),
        out_shape=jax.ShapeDtypeStruct((idx.shape[0], heads, BLOCK, 2 * BLOCK), F32),
        compiler_params=_cparams("arbitrary", "arbitrary"),
        name="bias_tables",
    )(rel_bias, idx)


def _dilated_attn_kernel(q_ref, k_ref, v_ref, bias_ref, o_ref, m_ref, l_ref, u_ref,
                         s0_ref, s1_ref, *, group):
    seq = q_ref.shape[0]
    lane = lax.broadcasted_iota(jnp.int32, (1, LANES), 1)
    head_mask = (lane < HEAD_DIM, lane >= HEAD_DIM)
    nt_dims = (((1,), (1,)), ((), ()))
    ngroups = seq // BLOCK // group
    assert ngroups % 2 == 0

    for g, (_, dil) in enumerate(DA_PAIRS):
        nb = seq // dil // BLOCK

        def rows(start):
            if dil == 1:
                return pl.ds(pl.multiple_of(start, BLOCK), BLOCK)
            return pl.ds(start, BLOCK, stride=dil)

        def block_rows(blk):
            r = blk // nb
            j = blk % nb
            cur = rows(r + j * (BLOCK * dil))
            prev = rows(r + jnp.maximum(j - 1, 0) * (BLOCK * dil))
            return cur, prev, 2 * g + (j == 0).astype(jnp.int32)

        def logits(grp, s_ref):
            for t in range(group):
                cur, prev, table = block_rows(grp * group + t)
                q = q_ref[cur, :] * (1.0 / math.sqrt(HEAD_DIM))
                k2 = jnp.concatenate([k_ref[prev, :], k_ref[cur, :]], axis=0).astype(BF16)
                for h in range(HEADS_PER_TILE):
                    qh = jnp.where(head_mask[h], q, 0.0).astype(BF16)
                    s = lax.dot_general(qh, k2, nt_dims, preferred_element_type=F32)
                    s_ref[HEADS_PER_TILE * t + h] = s + bias_ref[table, h]

        def finish(grp, s_ref):
            for t in range(group):
                cur, prev, _ = block_rows(grp * group + t)
                v2 = jnp.concatenate([v_ref[prev, :], v_ref[cur, :]], axis=0).astype(BF16)
                v2 = jnp.concatenate([v2, jnp.ones(v2.shape, BF16)], axis=1)
                stats = []
                for h in range(HEADS_PER_TILE):
                    s = s_ref[HEADS_PER_TILE * t + h]
                    m = jnp.max(s, axis=-1, keepdims=True)
                    p = jnp.exp(s - m)
                    pv = jnp.dot(p.astype(BF16), v2, preferred_element_type=F32)
                    stats.append((jnp.broadcast_to(m, (BLOCK, LANES)), pv[:, LANES:],
                                  pv[:, :LANES]))
                both = [jnp.where(head_mask[0], a, b) for a, b in zip(*stats)]
                m_ref[g, cur, :] = both[0]
                l_ref[g, cur, :] = both[1]
                u_ref[g, cur, :] = both[2]

        def body(i, carry):
            logits(2 * i + 1, s1_ref)
            finish(2 * i, s0_ref)
            logits(jnp.minimum(2 * i + 2, ngroups - 1), s0_ref)
            finish(2 * i + 1, s1_ref)
            return carry

        logits(jnp.int32(0), s0_ref)
        lax.fori_loop(0, ngroups // 2, body, 0)

    def merge(c, carry):
        rs = pl.ds(pl.multiple_of(c * BLOCK, BLOCK), BLOCK)
        m = m_ref[:, rs, :]
        wgt = jnp.exp(m - jnp.max(m, axis=0, keepdims=True))
        num = jnp.sum(wgt * u_ref[:, rs, :], axis=0)
        den = jnp.sum(wgt * l_ref[:, rs, :], axis=0)
        o_ref[rs, :] = (num / den).astype(o_ref.dtype)
        return carry

    lax.fori_loop(0, seq // BLOCK, merge, 0)


def dilated_attention(qkv, bias, batch, seq, group=4):
    n = qkv.shape[0]
    width = qkv.shape[1] // 3
    nhp = width // LANES
    blk = lambda off: pl.BlockSpec((seq, LANES), lambda b, p: (b, off + p))
    logits_buf = pltpu.VMEM((group * HEADS_PER_TILE, BLOCK, 2 * BLOCK), F32)
    return pl.pallas_call(
        functools.partial(_dilated_attn_kernel, group=group),
        grid=(batch, nhp),
        in_specs=[blk(0), blk(nhp), blk(2 * nhp),
                  pl.BlockSpec((bias.shape[0], HEADS_PER_TILE, BLOCK, 2 * BLOCK),
                               lambda b, p: (0, p, 0, 0))],
        out_specs=pl.BlockSpec((seq, LANES), lambda b, p: (b, p)),
        out_shape=jax.ShapeDtypeStruct((n, width), BF16),
        scratch_shapes=[pltpu.VMEM((len(DA_PAIRS), seq, LANES), F32)] * 3 + [logits_buf] * 2,
        compiler_params=_cparams("parallel", "parallel"),
        name="dilated_attention",
    )(qkv, qkv, qkv, bias)


def _conformer_conv_kernel(a_ref, gate_ref, w_ref, b_ref, lg_ref, lb_ref, o_ref,
                           pad_ref, sh_ref, wb_ref, *, taps, halo, chunk):
    ts = o_ref.shape[0]

    @pl.when(pl.program_id(1) == 0)
    def _():
        pad_ref[0:halo, :] = jnp.zeros((halo, pad_ref.shape[1]), F32)

    @pl.when(pl.program_id(1) != 0)
    def _():
        pad_ref[0:halo, :] = pad_ref[ts:ts + halo, :]

    pad_ref[halo:, :] = a_ref[...].astype(F32) * jax.nn.sigmoid(gate_ref[...].astype(F32))

    lead = halo - (taps - 1)
    n_a = [len(range(b, taps, SUBLANES)) for b in range(SUBLANES)]
    width = o_ref.shape[1]
    for b in range(SUBLANES):
        groups = ts // SUBLANES + n_a[b] - 1
        win = pad_ref[lead + b:lead + b + groups * SUBLANES, :]
        sh_ref[b, 0:groups] = win.reshape(groups, SUBLANES, width)
    for j in range(taps):
        wb_ref[j] = jnp.broadcast_to(w_ref[j:j + 1, :], (SUBLANES, width))

    cg = chunk // SUBLANES
    for c0 in range(0, ts, chunk):
        acc = jnp.zeros((cg, SUBLANES, width), F32)
        for b in range(SUBLANES):
            for a in range(n_a[b]):
                g0 = c0 // SUBLANES + a
                acc = acc + wb_ref[SUBLANES * a + b][None] * sh_ref[b, g0:g0 + cg]
        y = acc.reshape(chunk, width) + b_ref[...]
        mu = jnp.mean(y, axis=-1, keepdims=True)
        yc = y - mu
        var = jnp.mean(yc * yc, axis=-1, keepdims=True)
        z = yc * lax.rsqrt(var + EPS) * lg_ref[...] + lb_ref[...]
        o_ref[c0:c0 + chunk, :] = (z * jax.nn.sigmoid(z)).astype(o_ref.dtype)


def conformer_conv(p, w_cc, b_cc, ln_g, ln_b, batch, seq, ts=256, chunk=32):
    n = p.shape[0]
    taps, width = w_cc.shape
    halo = -(-(taps - 1) // SUBLANES) * SUBLANES
    nt = seq // ts
    vec = lambda v: v.reshape(1, width)
    vspec = pl.BlockSpec((1, width), lambda b, t: (0, 0))
    return pl.pallas_call(
        functools.partial(_conformer_conv_kernel, taps=taps, halo=halo, chunk=chunk),
        grid=(batch, nt),
        in_specs=[pl.BlockSpec((ts, width), lambda b, t: (b * nt + t, 0)),
                  pl.BlockSpec((ts, width), lambda b, t: (b * nt + t, 1)),
                  pl.BlockSpec(w_cc.shape, lambda b, t: (0, 0)),
                  vspec, vspec, vspec],
        out_specs=pl.BlockSpec((ts, width), lambda b, t: (b * nt + t, 0)),
        out_shape=jax.ShapeDtypeStruct((n, width), BF16),
        scratch_shapes=[pltpu.VMEM((ts + halo, width), F32),
                        pltpu.VMEM((SUBLANES, (ts + halo) // SUBLANES, SUBLANES, width), F32),
                        pltpu.VMEM((taps, SUBLANES, width), F32)],
        compiler_params=_cparams("arbitrary", "arbitrary"),
        name="conformer_conv",
    )(p, p, w_cc, vec(b_cc), vec(ln_g), vec(ln_b))


def _suffix_sum_matrix():
    j = np.arange(BLOCK)[:, None]
    s = np.arange(BLOCK)[None, :]
    half = -np.concatenate([(j >= s).astype(np.float32), np.ones((BLOCK, BLOCK), np.float32)],
                           axis=1)
    return np.concatenate([half, half], axis=0)


def _stick_breaking_kernel(q_ref, k_ref, v_ref, u_ref, o_ref, qm_ref, run_ref, acc_ref, cat_ref):
    tq = q_ref.shape[0]
    nsub = tq // BLOCK
    qi = pl.program_id(2)
    lane = lax.broadcasted_iota(jnp.int32, (1, LANES), 1)
    head_mask = (lane < HEAD_DIM, lane >= HEAD_DIM)
    nt_dims = (((1,), (1,)), ((), ()))
    rows2 = HEADS_PER_TILE * BLOCK
    row = lax.broadcasted_iota(jnp.int32, (rows2, BLOCK), 0)
    col = lax.broadcasted_iota(jnp.int32, (rows2, BLOCK), 1)
    before = col < jnp.bitwise_and(row, BLOCK - 1)

    for r in range(nsub):
        q = q_ref[r * BLOCK:(r + 1) * BLOCK, :].astype(F32) * (1.0 / math.sqrt(HEAD_DIM))
        qm_ref[r] = jnp.concatenate(
            [jnp.where(head_mask[h], q, 0.0) for h in range(HEADS_PER_TILE)], axis=0).astype(BF16)
    run_ref[...] = jnp.zeros(run_ref.shape, F32)
    acc_ref[...] = jnp.zeros(acc_ref.shape, F32)

    def step(d, own_block):
        kbs = [qi * nsub + r - d for r in range(nsub)]
        kss = [pl.ds(pl.multiple_of(jnp.maximum(kb, 0) * BLOCK, BLOCK), BLOCK) for kb in kbs]
        zs = [lax.dot_general(qm_ref[r], k_ref[kss[r], :], nt_dims, preferred_element_type=F32)
              for r in range(nsub)]
        for r, z in enumerate(zs):
            neg_log_rest = jnp.maximum(z, 0.0) + jnp.log(1.0 + jnp.exp(-jnp.abs(z)))
            if own_block:
                neg_log_rest = jnp.where(before, neg_log_rest, 0.0)
            hi = neg_log_rest.astype(BF16)
            lo = (neg_log_rest - hi.astype(F32)).astype(BF16)
            cat_ref[r * rows2:(r + 1) * rows2, :] = jnp.concatenate([hi, lo], axis=1)
        sums = jnp.dot(cat_ref[...], u_ref[...], preferred_element_type=F32)
        weights = []
        for r in range(nsub):
            sub = sums[r * rows2:(r + 1) * rows2]
            run = run_ref[r]
            if not own_block:
                run = jnp.where(kbs[r] < 0, MASKED, run)
            a = jnp.exp(zs[r] + sub[:, 0:BLOCK] + run)
            if own_block:
                a = jnp.where(before, a, 0.0)
            weights.append(a.astype(BF16))
            run_ref[r] = run + sub[:, BLOCK:2 * BLOCK]
        for r in range(nsub):
            acc_ref[r] += jnp.dot(weights[r], v_ref[kss[r], :], preferred_element_type=F32)

    def still_open():
        return jnp.max(run_ref[...]) > SB_EXIT

    def body(c):
        d, _ = c
        step(d, False)
        return d + 1, still_open()

    step(0, True)
    lax.while_loop(lambda c: c[1], body, (jnp.int32(1), still_open()))
    for r in range(nsub):
        o_ref[r * BLOCK:(r + 1) * BLOCK, :] = jnp.where(
            head_mask[0], acc_ref[r, 0:BLOCK, :], acc_ref[r, BLOCK:rows2, :]).astype(o_ref.dtype)


def stick_breaking(p, col0, batch, seq, tq=1024):
    n = p.shape[0]
    width = (p.shape[1] - col0) // 3
    nhp = width // LANES
    c0 = col0 // LANES
    nq = seq // tq
    u = jnp.asarray(_suffix_sum_matrix(), BF16)
    return pl.pallas_call(
        _stick_breaking_kernel,
        grid=(batch, nhp, nq),
        in_specs=[pl.BlockSpec((tq, LANES), lambda b, p_, i: (b * nq + i, c0 + p_)),
                  pl.BlockSpec((seq, LANES), lambda b, p_, i: (b, c0 + nhp + p_)),
                  pl.BlockSpec((seq, LANES), lambda b, p_, i: (b, c0 + 2 * nhp + p_)),
                  pl.BlockSpec(u.shape, lambda b, p_, i: (0, 0))],
        out_specs=pl.BlockSpec((tq, LANES), lambda b, p_, i: (b * nq + i, p_)),
        out_shape=jax.ShapeDtypeStruct((n, width), BF16),
        scratch_shapes=[pltpu.VMEM((tq // BLOCK, HEADS_PER_TILE * BLOCK, LANES), BF16),
                        pltpu.VMEM((tq // BLOCK, HEADS_PER_TILE * BLOCK, LANES), F32),
                        pltpu.VMEM((tq // BLOCK, HEADS_PER_TILE * BLOCK, LANES), F32),
                        pltpu.VMEM((HEADS_PER_TILE * tq, 2 * BLOCK), BF16)],
        compiler_params=_cparams("parallel", "parallel", "arbitrary"),
        name="stick_breaking",
    )(p, p, p, u)


def _out_proj_kernel(ya_ref, yb_ref, wa_ref, wb_ref, g_ref, x_ref, o_ref):
    mix = (jnp.dot(ya_ref[...], wa_ref[...], preferred_element_type=F32)
           + jnp.dot(yb_ref[...], wb_ref[...], preferred_element_type=F32))
    o_ref[...] = x_ref[...] + _rms(mix, g_ref[...])


def out_proj(ya, yb, w_out, g, x, tm=512):
    n, d = x.shape
    wa_rows = ya.shape[1]
    wb_rows = yb.shape[1]
    return pl.pallas_call(
        _out_proj_kernel,
        grid=(n // tm,),
        in_specs=[pl.BlockSpec((tm, wa_rows), lambda i: (i, 0)),
                  pl.BlockSpec((tm, wb_rows), lambda i: (i, 0)),
                  pl.BlockSpec((wa_rows, d), lambda i: (0, 0)),
                  pl.BlockSpec((wb_rows, d), lambda i: (wa_rows // wb_rows, 0)),
                  pl.BlockSpec((1, d), lambda i: (0, 0)),
                  pl.BlockSpec((tm, d), lambda i: (i, 0))],
        out_specs=pl.BlockSpec((tm, d), lambda i: (i, 0)),
        out_shape=jax.ShapeDtypeStruct((n, d), F32),
        compiler_params=_cparams("parallel"),
        name="out_proj",
    )(ya, yb, w_out, w_out, g.reshape(1, d), x)


def _conv_ffn_kernel(x_ref, gin_ref, wup_ref, wc_ref, bc_ref, wd_ref, gout_ref, o_ref,
                     act_ref, pad_ref, carry_ref, *, tiles_per_seq, fc):
    tm = x_ref.shape[0]
    d_ff = wd_ref.shape[0]
    x = x_ref[...]
    h = _rms(x, gin_ref[...]).astype(BF16)

    @pl.when(pl.program_id(0) % tiles_per_seq == 0)
    def _():
        carry_ref[...] = jnp.zeros(carry_ref.shape, F32)

    for n, c0 in enumerate(range(0, d_ff, fc)):
        cols = slice(c0, c0 + fc)
        gate = jnp.dot(h, wup_ref[:, cols], preferred_element_type=F32)
        up = jnp.dot(h, wup_ref[:, d_ff + c0:d_ff + c0 + fc], preferred_element_type=F32)
        pad = pad_ref.at[n % pad_ref.shape[0]]
        pad[0:SUBLANES, :] = carry_ref[:, cols]
        pad[SUBLANES:, :] = gate
        carry_ref[:, cols] = gate[tm - SUBLANES:, :]
        conv = (wc_ref[2:3, cols] * gate
                + wc_ref[1:2, cols] * pad[SUBLANES - 1:SUBLANES - 1 + tm, :]
                + wc_ref[0:1, cols] * pad[SUBLANES - 2:SUBLANES - 2 + tm, :]
                + bc_ref[:, cols])
        act_ref[:, cols] = (conv * jax.nn.sigmoid(conv) * up).astype(BF16)
    out = jnp.dot(act_ref[...], wd_ref[...], preferred_element_type=F32)
    o_ref[...] = x + _rms(out, gout_ref[...])


def conv_ffn(x, g_in, w_up, w_conv, b_conv, w_down, g_out, seq, tm=512, fc=256):
    n, d = x.shape
    d_ff = w_down.shape[0]
    assert d_ff % fc == 0 and seq % tm == 0
    const = lambda shape: pl.BlockSpec(shape, lambda i: (0, 0), pipeline_mode=pl.Buffered(1))
    return pl.pallas_call(
        functools.partial(_conv_ffn_kernel, tiles_per_seq=seq // tm, fc=fc),
        grid=(n // tm,),
        in_specs=[pl.BlockSpec((tm, d), lambda i: (i, 0)),
                  const((1, d)),
                  const(w_up.shape),
                  const(w_conv.shape),
                  const((1, d_ff)),
                  const(w_down.shape),
                  const((1, d))],
        out_specs=pl.BlockSpec((tm, d), lambda i: (i, 0)),
        out_shape=jax.ShapeDtypeStruct((n, d), F32),
        scratch_shapes=[pltpu.VMEM((tm, d_ff), BF16),
                        pltpu.VMEM((2, tm + SUBLANES, fc), F32),
                        pltpu.VMEM((SUBLANES, d_ff), F32)],
        compiler_params=_cparams("arbitrary"),
        name="conv_ffn",
    )(x, g_in.reshape(1, d), w_up, w_conv, b_conv.reshape(1, d_ff), w_down,
      g_out.reshape(1, d))


def kernel(x, norm_g, rel_bias, w_in_even, w_out_even, w_sc, w_in_odd, w_out_odd, w_cc, b_cc,
           ln_cc_g, ln_cc_b, w_up, w_ffn_conv, b_ffn_conv, w_down):
    batch, seq, d = x.shape
    depth = norm_g.shape[0]
    sc_width = w_sc.shape[2]
    cc_width = w_cc.shape[2]
    xf = x.reshape(batch * seq, d)
    bias = bias_tables(rel_bias)
    for layer in range(depth):
        i = layer // 2
        if layer % 2 == 0:
            w_in = w_in_even[i].astype(BF16)
            conv_cols = 3 * sc_width
            pa, qkv = norm_matmul(xf, norm_g[layer, 0], w_in,
                                  [(conv_cols, BF16), (w_in.shape[1] - conv_cols, F32)])
            y_a = short_conv(pa, w_sc[i], batch, seq)
            y_b = dilated_attention(qkv, bias, batch, seq)
            w_out = w_out_even[i].astype(BF16)
        else:
            w_in = w_in_odd[i].astype(BF16)
            (p,) = norm_matmul(xf, norm_g[layer, 0], w_in, [(w_in.shape[1], BF16)])
            y_a = conformer_conv(p, w_cc[i], b_cc[i], ln_cc_g[i], ln_cc_b[i], batch, seq)
            y_b = stick_breaking(p, 2 * cc_width, batch, seq)
            w_out = w_out_odd[i].astype(BF16)
        xf = out_proj(y_a, y_b, w_out, norm_g[layer, 1], xf)
        xf = conv_ffn(xf, norm_g[layer, 2], w_up[layer].astype(BF16), w_ffn_conv[layer],
                      b_ffn_conv[layer], w_down[layer].astype(BF16), norm_g[layer, 3], seq)
    return xf.reshape(batch, seq, d)
```

```python
import functools
import math

import numpy as np
import jax
import jax.numpy as jnp
from jax import lax
from jax.experimental import pallas as pl
from jax.experimental.pallas import tpu as pltpu

F32 = jnp.float32
BF16 = jnp.bfloat16

EPS = 1e-6
HEAD_DIM = 64
LANES = 128
SUBLANES = 8
HEADS_PER_TILE = LANES // HEAD_DIM
DA_PAIRS = ((128, 1), (512, 4), (2048, 16))
DA_SPAN = 128
BLOCK = 128
REL_BUCKETS = 32
REL_MAX_DIST = 2048
MASKED = -1e30
SB_EXIT = -105.0
VMEM_LIMIT = 56 * 1024 * 1024


def _cparams(*sem):
    return pltpu.CompilerParams(dimension_semantics=sem, vmem_limit_bytes=VMEM_LIMIT)


def _rms(x, g):
    return x * lax.rsqrt(jnp.mean(x * x, axis=-1, keepdims=True) + EPS) * g


def _norm_matmul_kernel(x_ref, g_ref, w_ref, *o_refs, col_chunk):
    h = _rms(x_ref[...], g_ref[...]).astype(BF16)
    col = 0
    for o_ref in o_refs:
        width = o_ref.shape[1]
        for c in range(0, width, col_chunk):
            o_ref[:, c:c + col_chunk] = jnp.dot(
                h, w_ref[:, col + c:col + c + col_chunk],
                preferred_element_type=F32).astype(o_ref.dtype)
        col += width


def norm_matmul(x, g, w, outs, tm=512, col_chunk=512):
    n, d = x.shape
    assert n % tm == 0 and sum(wd for wd, _ in outs) == w.shape[1]
    assert all(wd % col_chunk == 0 for wd, _ in outs)
    return pl.pallas_call(
        functools.partial(_norm_matmul_kernel, col_chunk=col_chunk),
        grid=(n // tm,),
        in_specs=[pl.BlockSpec((tm, d), lambda i: (i, 0)),
                  pl.BlockSpec((1, d), lambda i: (0, 0)),
                  pl.BlockSpec(w.shape, lambda i: (0, 0))],
        out_specs=[pl.BlockSpec((tm, wd), lambda i: (i, 0)) for wd, _ in outs],
        out_shape=[jax.ShapeDtypeStruct((n, wd), dt) for wd, dt in outs],
        compiler_params=_cparams("parallel"),
        name="norm_in_proj",
    )(x, g.reshape(1, d), w)


def _short_conv_kernel(gb_ref, gc_ref, xa_ref, w_ref, o_ref, pad_ref):
    ts = o_ref.shape[0]
    @pl.when(pl.program_id(1) == 0)
    def _():
        pad_ref[0:SUBLANES, :] = jnp.zeros((SUBLANES, pad_ref.shape[1]), F32)

    @pl.when(pl.program_id(1) != 0)
    def _():
        pad_ref[0:SUBLANES, :] = pad_ref[ts:ts + SUBLANES, :]

    c = gc_ref[...].astype(F32) * xa_ref[...].astype(F32)
    pad_ref[SUBLANES:, :] = c
    conv = (w_ref[2:3, :] * c
            + w_ref[1:2, :] * pad_ref[SUBLANES - 1:SUBLANES - 1 + ts, :]
            + w_ref[0:1, :] * pad_ref[SUBLANES - 2:SUBLANES - 2 + ts, :])
    o_ref[...] = (gb_ref[...].astype(F32) * conv).astype(o_ref.dtype)


def short_conv(pa, w_sc, batch, seq, ts=512):
    n = pa.shape[0]
    width = w_sc.shape[1]
    nt = seq // ts
    return pl.pallas_call(
        _short_conv_kernel,
        grid=(batch, nt),
        in_specs=[pl.BlockSpec((ts, width), lambda b, t: (b * nt + t, 0)),
                  pl.BlockSpec((ts, width), lambda b, t: (b * nt + t, 1)),
                  pl.BlockSpec((ts, width), lambda b, t: (b * nt + t, 2)),
                  pl.BlockSpec(w_sc.shape, lambda b, t: (0, 0))],
        out_specs=pl.BlockSpec((ts, width), lambda b, t: (b * nt + t, 0)),
        out_shape=jax.ShapeDtypeStruct((n, width), BF16),
        scratch_shapes=[pltpu.VMEM((ts + SUBLANES, width), F32)],
        compiler_params=_cparams("arbitrary", "arbitrary"),
        name="short_conv",
    )(pa, pa, pa, w_sc)


def _bucket_tables():
    rel = np.arange(BLOCK)[:, None] - np.arange(2 * BLOCK)[None, :] + DA_SPAN
    valid = (rel >= 0) & (rel <= DA_SPAN)
    max_exact = REL_BUCKETS // 2
    tabs = []
    for _, dil in DA_PAIRS:
        dist = np.clip(rel, 0, DA_SPAN) * dil
        d = np.maximum(dist, 1).astype(np.float32)
        large = max_exact + (np.log(d / np.float32(max_exact))
                             / np.float32(math.log(REL_MAX_DIST / max_exact))
                             * np.float32(REL_BUCKETS - max_exact)).astype(np.int32)
        large = np.minimum(large, REL_BUCKETS - 1)
        bucket = np.where(dist < max_exact, dist, large)
        tabs.append(np.where(valid, bucket, -1))
        tabs.append(np.where(valid & (np.arange(2 * BLOCK)[None, :] >= BLOCK), bucket, -1))
    return np.stack(tabs).astype(np.int32)


def _bias_table_kernel(rb_ref, idx_ref, o_ref):
    h = pl.program_id(1)
    idx = idx_ref[0]
    acc = jnp.full(idx.shape, MASKED, F32)
    for b in range(REL_BUCKETS):
        acc = jnp.where(idx == b, rb_ref[b, h], acc)
    o_ref[0, 0] = acc


def bias_tables(rel_bias):
    idx = jnp.asarray(_bucket_tables())
    heads = rel_bias.shape[1]
    return pl.pallas_call(
        _bias_table_kernel,
        grid=(idx.shape[0], heads),
        in_specs=[pl.BlockSpec(memory_space=pltpu.SMEM),
                  pl.BlockSpec((1, BLOCK, 2 * BLOCK), lambda g, h: (g, 0, 0))],
        out_specs=pl.BlockSpec((1, 1, BLOCK, 2 * BLOCK), lambda g, h: (g, h, 0, 0)),
        out_shape=jax.ShapeDtypeStruct((idx.shape[0], heads, BLOCK, 2 * BLOCK), F32),
        compiler_params=_cparams("arbitrary", "arbitrary"),
        name="bias_tables",
    )(rel_bias, idx)


def _dilated_attn_kernel(q_ref, k_ref, v_ref, bias_ref, o_ref, m_ref, l_ref, u_ref,
                         s0_ref, s1_ref, *, group):
    seq = q_ref.shape[0]
    lane = lax.broadcasted_iota(jnp.int32, (1, LANES), 1)
    head_mask = (lane < HEAD_DIM, lane >= HEAD_DIM)
    nt_dims = (((1,), (1,)), ((), ()))
    ngroups = seq // BLOCK // group
    assert ngroups % 2 == 0

    for g, (_, dil) in enumerate(DA_PAIRS):
        nb = seq // dil // BLOCK

        def rows(start):
            if dil == 1:
                return pl.ds(pl.multiple_of(start, BLOCK), BLOCK)
            return pl.ds(start, BLOCK, stride=dil)

        def block_rows(blk):
            r = blk // nb
            j = blk % nb
            cur = rows(r + j * (BLOCK * dil))
            prev = rows(r + jnp.maximum(j - 1, 0) * (BLOCK * dil))
            return cur, prev, 2 * g + jnp.where(j == 0, 1, 0)

        def logits(grp, s_ref):
            for t in range(group):
                cur, prev, table = block_rows(grp * group + t)
                q = q_ref[cur, :] * (1.0 / math.sqrt(HEAD_DIM))
                k2 = jnp.concatenate([k_ref[prev, :], k_ref[cur, :]], axis=0).astype(BF16)
                for h in range(HEADS_PER_TILE):
                    qh = jnp.where(head_mask[h], q, 0.0).astype(BF16)
                    s = lax.dot_general(qh, k2, nt_dims, preferred_element_type=F32)
                    s_ref[HEADS_PER_TILE * t + h] = s + bias_ref[table, h]

        def finish(grp, s_ref):
            for t in range(group):
                cur, prev, _ = block_rows(grp * group + t)
                v2 = jnp.concatenate([v_ref[prev, :], v_ref[cur, :]], axis=0).astype(BF16)
                v2 = jnp.concatenate([v2, jnp.ones(v2.shape, BF16)], axis=1)
                stats = []
                for h in range(HEADS_PER_TILE):
                    s = s_ref[HEADS_PER_TILE * t + h]
                    m = jnp.max(s, axis=-1, keepdims=True)
                    p = jnp.exp(s - m)
                    pv = jnp.dot(p.astype(BF16), v2, preferred_element_type=F32)
                    stats.append((jnp.broadcast_to(m, (BLOCK, LANES)), pv[:, LANES:],
                                  pv[:, :LANES]))
                both = [jnp.where(head_mask[0], a, b) for a, b in zip(*stats)]
                m_ref[g, cur, :] = both[0]
                l_ref[g, cur, :] = both[1]
                u_ref[g, cur, :] = both[2]

        def body(i, carry):
            logits(2 * i + 1, s1_ref)
            finish(2 * i, s0_ref)
            logits(jnp.minimum(2 * i + 2, ngroups - 1), s0_ref)
            finish(2 * i + 1, s1_ref)
            return carry

        logits(jnp.int32(0), s0_ref)
        lax.fori_loop(0, ngroups // 2, body, 0)

    def merge(c, carry):
        rs = pl.ds(pl.multiple_of(c * BLOCK, BLOCK), BLOCK)
        m = m_ref[:, rs, :]
        wgt = jnp.exp(m - jnp.max(m, axis=0, keepdims=True))
        num = jnp.sum(wgt * u_ref[:, rs, :], axis=0)
        den = jnp.sum(wgt * l_ref[:, rs, :], axis=0)
        o_ref[rs, :] = (num / den).astype(o_ref.dtype)
        return carry

    lax.fori_loop(0, seq // BLOCK, merge, 0)


def dilated_attention(qkv, bias, batch, seq, group=4):
    n = qkv.shape[0]
    width = qkv.shape[1] // 3
    nhp = width // LANES
    blk = lambda off: pl.BlockSpec((seq, LANES), lambda b, p: (b, off + p))
    logits_buf = pltpu.VMEM((group * HEADS_PER_TILE, BLOCK, 2 * BLOCK), F32)
    return pl.pallas_call(
        functools.partial(_dilated_attn_kernel, group=group),
        grid=(batch, nhp),
        in_specs=[blk(0), blk(nhp), blk(2 * nhp),
                  pl.BlockSpec((bias.shape[0], HEADS_PER_TILE, BLOCK, 2 * BLOCK),
                               lambda b, p: (0, p, 0, 0))],
        out_specs=pl.BlockSpec((seq, LANES), lambda b, p: (b, p)),
        out_shape=jax.ShapeDtypeStruct((n, width), BF16),
        scratch_shapes=[pltpu.VMEM((len(DA_PAIRS), seq, LANES), F32)] * 3 + [logits_buf] * 2,
        compiler_params=_cparams("parallel", "parallel"),
        name="dilated_attention",
    )(qkv, qkv, qkv, bias)


def _conformer_conv_kernel(a_ref, gate_ref, w_ref, b_ref, lg_ref, lb_ref, o_ref,
                           pad_ref, sh_ref, wb_ref, *, taps, halo, chunk):
    ts = o_ref.shape[0]

    @pl.when(pl.program_id(1) == 0)
    def _():
        pad_ref[0:halo, :] = jnp.zeros((halo, pad_ref.shape[1]), F32)

    @pl.when(pl.program_id(1) != 0)
    def _():
        pad_ref[0:halo, :] = pad_ref[ts:ts + halo, :]

    pad_ref[halo:, :] = a_ref[...].astype(F32) * jax.nn.sigmoid(gate_ref[...].astype(F32))

    lead = halo - (taps - 1)
    n_a = [len(range(b, taps, SUBLANES)) for b in range(SUBLANES)]
    width = o_ref.shape[1]
    for b in range(SUBLANES):
        groups = ts // SUBLANES + n_a[b] - 1
        win = pad_ref[lead + b:lead + b + groups * SUBLANES, :]
        sh_ref[b, 0:groups] = win.reshape(groups, SUBLANES, width)
    for j in range(taps):
        wb_ref[j] = jnp.broadcast_to(w_ref[j:j + 1, :], (SUBLANES, width))

    cg = chunk // SUBLANES
    for c0 in range(0, ts, chunk):
        acc = jnp.zeros((cg, SUBLANES, width), F32)
        for b in range(SUBLANES):
            for a in range(n_a[b]):
                g0 = c0 // SUBLANES + a
                acc = acc + wb_ref[SUBLANES * a + b][None] * sh_ref[b, g0:g0 + cg]
        y = acc.reshape(chunk, width) + b_ref[...]
        mu = jnp.mean(y, axis=-1, keepdims=True)
        yc = y - mu
        var = jnp.mean(yc * yc, axis=-1, keepdims=True)
        z = yc * lax.rsqrt(var + EPS) * lg_ref[...] + lb_ref[...]
        o_ref[c0:c0 + chunk, :] = (z * jax.nn.sigmoid(z)).astype(o_ref.dtype)


def conformer_conv(p, w_cc, b_cc, ln_g, ln_b, batch, seq, ts=256, chunk=32):
    n = p.shape[0]
    taps, width = w_cc.shape
    halo = -(-(taps - 1) // SUBLANES) * SUBLANES
    nt = seq // ts
    vec = lambda v: v.reshape(1, width)
    vspec = pl.BlockSpec((1, width), lambda b, t: (0, 0))
    return pl.pallas_call(
        functools.partial(_conformer_conv_kernel, taps=taps, halo=halo, chunk=chunk),
        grid=(batch, nt),
        in_specs=[pl.BlockSpec((ts, width), lambda b, t: (b * nt + t, 0)),
                  pl.BlockSpec((ts, width), lambda b, t: (b * nt + t, 1)),
                  pl.BlockSpec(w_cc.shape, lambda b, t: (0, 0)),
                  vspec, vspec, vspec],
        out_specs=pl.BlockSpec((ts, width), lambda b, t: (b * nt + t, 0)),
        out_shape=jax.ShapeDtypeStruct((n, width), BF16),
        scratch_shapes=[pltpu.VMEM((ts + halo, width), F32),
                        pltpu.VMEM((SUBLANES, (ts + halo) // SUBLANES, SUBLANES, width), F32),
                        pltpu.VMEM((taps, SUBLANES, width), F32)],
        compiler_params=_cparams("arbitrary", "arbitrary"),
        name="conformer_conv",
    )(p, p, w_cc, vec(b_cc), vec(ln_g), vec(ln_b))


def _suffix_sum_matrix():
    j = np.arange(BLOCK)[:, None]
    s = np.arange(BLOCK)[None, :]
    half = -np.concatenate([(j >= s).astype(np.float32), np.ones((BLOCK, BLOCK), np.float32)],
                           axis=1)
    return np.concatenate([half, half], axis=0)


def _stick_breaking_kernel(q_ref, k_ref, v_ref, u_ref, o_ref, qm_ref, run_ref, acc_ref, cat_ref):
    tq = q_ref.shape[0]
    nsub = tq // BLOCK
    qi = pl.program_id(2)
    lane = lax.broadcasted_iota(jnp.int32, (1, LANES), 1)
    head_mask = (lane < HEAD_DIM, lane >= HEAD_DIM)
    nt_dims = (((1,), (1,)), ((), ()))
    rows2 = HEADS_PER_TILE * BLOCK
    row = lax.broadcasted_iota(jnp.int32, (rows2, BLOCK), 0)
    col = lax.broadcasted_iota(jnp.int32, (rows2, BLOCK), 1)
    before = col < jnp.bitwise_and(row, BLOCK - 1)

    for r in range(nsub):
        q = q_ref[r * BLOCK:(r + 1) * BLOCK, :].astype(F32) * (1.0 / math.sqrt(HEAD_DIM))
        qm_ref[r] = jnp.concatenate(
            [jnp.where(head_mask[h], q, 0.0) for h in range(HEADS_PER_TILE)], axis=0).astype(BF16)
    run_ref[...] = jnp.zeros(run_ref.shape, F32)
    acc_ref[...] = jnp.zeros(acc_ref.shape, F32)

    def step(d, own_block):
        kbs = [qi * nsub + r - d for r in range(nsub)]
        kss = [pl.ds(pl.multiple_of(jnp.maximum(kb, 0) * BLOCK, BLOCK), BLOCK) for kb in kbs]
        zs = [lax.dot_general(qm_ref[r], k_ref[kss[r], :], nt_dims, preferred_element_type=F32)
              for r in range(nsub)]
        for r, z in enumerate(zs):
            neg_log_rest = jnp.maximum(z, 0.0) + jnp.log(1.0 + jnp.exp(-jnp.abs(z)))
            if own_block:
                neg_log_rest = jnp.where(before, neg_log_rest, 0.0)
            hi = neg_log_rest.astype(BF16)
            lo = (neg_log_rest - hi.astype(F32)).astype(BF16)
            cat_ref[r * rows2:(r + 1) * rows2, :] = jnp.concatenate([hi, lo], axis=1)
        sums = jnp.dot(cat_ref[...], u_ref[...], preferred_element_type=F32)
        weights = []
        for r in range(nsub):
            sub = sums[r * rows2:(r + 1) * rows2]
            run = run_ref[r]
            if not own_block:
                run = jnp.where(kbs[r] < 0, MASKED, run)
            a = jnp.exp(zs[r] + sub[:, 0:BLOCK] + run)
            if own_block:
                a = jnp.where(before, a, 0.0)
            weights.append(a.astype(BF16))
            run_ref[r] = run + sub[:, BLOCK:2 * BLOCK]
        for r in range(nsub):
            acc_ref[r] += jnp.dot(weights[r], v_ref[kss[r], :], preferred_element_type=F32)

    def still_open():
        return jnp.max(run_ref[...]) > SB_EXIT

    def body(c):
        d, _ = c
        step(d, False)
        return d + 1, still_open()

    step(0, True)
    lax.while_loop(lambda c: c[1], body, (jnp.int32(1), still_open()))
    for r in range(nsub):
        o_ref[r * BLOCK:(r + 1) * BLOCK, :] = jnp.where(
            head_mask[0], acc_ref[r, 0:BLOCK, :], acc_ref[r, BLOCK:rows2, :]).astype(o_ref.dtype)


def stick_breaking(p, col0, batch, seq, tq=1024):
    n = p.shape[0]
    width = (p.shape[1] - col0) // 3
    nhp = width // LANES
    c0 = col0 // LANES
    nq = seq // tq
    u = jnp.asarray(_suffix_sum_matrix(), BF16)
    return pl.pallas_call(
        _stick_breaking_kernel,
        grid=(batch, nhp, nq),
        in_specs=[pl.BlockSpec((tq, LANES), lambda b, p_, i: (b * nq + i, c0 + p_)),
                  pl.BlockSpec((seq, LANES), lambda b, p_, i: (b, c0 + nhp + p_)),
                  pl.BlockSpec((seq, LANES), lambda b, p_, i: (b, c0 + 2 * nhp + p_)),
                  pl.BlockSpec(u.shape, lambda b, p_, i: (0, 0))],
        out_specs=pl.BlockSpec((tq, LANES), lambda b, p_, i: (b * nq + i, p_)),
        out_shape=jax.ShapeDtypeStruct((n, width), BF16),
        scratch_shapes=[pltpu.VMEM((tq // BLOCK, HEADS_PER_TILE * BLOCK, LANES), BF16),
                        pltpu.VMEM((tq // BLOCK, HEADS_PER_TILE * BLOCK, LANES), F32),
                        pltpu.VMEM((tq // BLOCK, HEADS_PER_TILE * BLOCK, LANES), F32),
                        pltpu.VMEM((HEADS_PER_TILE * tq, 2 * BLOCK), BF16)],
        compiler_params=_cparams("parallel", "parallel", "arbitrary"),
        name="stick_breaking",
    )(p, p, p, u)


def _layer_tail_kernel(ya_ref, yb_ref, x_ref, wo_ref, g_ref, wup_ref, wc_ref, bc_ref, wd_ref,
                       *rest, tiles_per_seq, fc, col_chunk, has_next):
    if has_next:
        gn_ref, win_ref, o_ref, *p_refs, act_ref, pad_ref, carry_ref = rest
    else:
        o_ref, act_ref, pad_ref, carry_ref = rest
        p_refs = []
    tm = x_ref.shape[0]
    d_ff = wd_ref.shape[0]
    wa_rows = ya_ref.shape[1]

    mix = (jnp.dot(ya_ref[...], wo_ref[0:wa_rows, :], preferred_element_type=F32)
           + jnp.dot(yb_ref[...], wo_ref[wa_rows:, :], preferred_element_type=F32))
    x = x_ref[...] + _rms(mix, g_ref[0:1, :])
    h = _rms(x, g_ref[1:2, :]).astype(BF16)

    @pl.when(pl.program_id(0) % tiles_per_seq == 0)
    def _():
        carry_ref[...] = jnp.zeros(carry_ref.shape, F32)

    for n, c0 in enumerate(range(0, d_ff, fc)):
        cols = slice(c0, c0 + fc)
        gate = jnp.dot(h, wup_ref[:, cols], preferred_element_type=F32)
        up = jnp.dot(h, wup_ref[:, d_ff + c0:d_ff + c0 + fc], preferred_element_type=F32)
        pad = pad_ref.at[n % pad_ref.shape[0]]
        pad[0:SUBLANES, :] = carry_ref[:, cols]
        pad[SUBLANES:, :] = gate
        carry_ref[:, cols] = gate[tm - SUBLANES:, :]
        conv = (wc_ref[2:3, cols] * gate
                + wc_ref[1:2, cols] * pad[SUBLANES - 1:SUBLANES - 1 + tm, :]
                + wc_ref[0:1, cols] * pad[SUBLANES - 2:SUBLANES - 2 + tm, :]
                + bc_ref[:, cols])
        act_ref[:, cols] = (conv * jax.nn.sigmoid(conv) * up).astype(BF16)
    out = jnp.dot(act_ref[...], wd_ref[...], preferred_element_type=F32)
    x = x + _rms(out, g_ref[2:3, :])
    o_ref[...] = x

    if has_next:
        hn = _rms(x, gn_ref[...]).astype(BF16)
        col = 0
        for p_ref in p_refs:
            width = p_ref.shape[1]
            for c in range(0, width, col_chunk):
                p_ref[:, c:c + col_chunk] = jnp.dot(
                    hn, win_ref[:, col + c:col + c + col_chunk],
                    preferred_element_type=F32).astype(p_ref.dtype)
            col += width


def layer_tail(ya, yb, x, w_out, gains, w_up, w_conv, b_conv, w_down, seq,
               next_gain=None, next_w_in=None, next_outs=(), tm=512, fc=256, col_chunk=512):
    n, d = x.shape
    d_ff = w_down.shape[0]
    has_next = next_w_in is not None
    assert d_ff % fc == 0 and seq % tm == 0
    const = lambda shape: pl.BlockSpec(shape, lambda i: (0, 0), pipeline_mode=pl.Buffered(1))
    row = lambda width: pl.BlockSpec((tm, width), lambda i: (i, 0))
    in_specs = [row(ya.shape[1]), row(yb.shape[1]), row(d), const(w_out.shape), const((3, d)),
                const(w_up.shape), const(w_conv.shape), const((1, d_ff)), const(w_down.shape)]
    args = [ya, yb, x, w_out, gains, w_up, w_conv, b_conv.reshape(1, d_ff), w_down]
    out_specs = [row(d)]
    out_shape = [jax.ShapeDtypeStruct((n, d), F32)]
    if has_next:
        in_specs += [const((1, d)), const(next_w_in.shape)]
        args += [next_gain.reshape(1, d), next_w_in]
        out_specs += [row(wd) for wd, _ in next_outs]
        out_shape += [jax.ShapeDtypeStruct((n, wd), dt) for wd, dt in next_outs]
    return pl.pallas_call(
        functools.partial(_layer_tail_kernel, tiles_per_seq=seq // tm, fc=fc,
                          col_chunk=col_chunk, has_next=has_next),
        grid=(n // tm,),
        in_specs=in_specs,
        out_specs=out_specs,
        out_shape=out_shape,
        scratch_shapes=[pltpu.VMEM((tm, d_ff), BF16),
                        pltpu.VMEM((2, tm + SUBLANES, fc), F32),
                        pltpu.VMEM((SUBLANES, d_ff), F32)],
        compiler_params=_cparams("arbitrary"),
        name="layer_tail",
    )(*args)


def kernel(x, norm_g, rel_bias, w_in_even, w_out_even, w_sc, w_in_odd, w_out_odd, w_cc, b_cc,
           ln_cc_g, ln_cc_b, w_up, w_ffn_conv, b_ffn_conv, w_down):
    batch, seq, d = x.shape
    depth = norm_g.shape[0]
    sc_width = w_sc.shape[2]
    cc_width = w_cc.shape[2]
    xf = x.reshape(batch * seq, d)
    bias = bias_tables(rel_bias)

    def in_proj_weights(layer):
        if layer % 2 == 0:
            w_in = w_in_even[layer // 2].astype(BF16)
            conv_cols = 3 * sc_width
            return w_in, [(conv_cols, BF16), (w_in.shape[1] - conv_cols, F32)]
        w_in = w_in_odd[layer // 2].astype(BF16)
        return w_in, [(w_in.shape[1], BF16)]

    w_in, outs = in_proj_weights(0)
    proj = norm_matmul(xf, norm_g[0, 0], w_in, outs)
    for layer in range(depth):
        i = layer // 2
        if layer % 2 == 0:
            pa, qkv = proj
            y_a = short_conv(pa, w_sc[i], batch, seq)
            y_b = dilated_attention(qkv, bias, batch, seq)
            w_out = w_out_even[i].astype(BF16)
        else:
            (p,) = proj
            y_a = conformer_conv(p, w_cc[i], b_cc[i], ln_cc_g[i], ln_cc_b[i], batch, seq)
            y_b = stick_breaking(p, 2 * cc_width, batch, seq)
            w_out = w_out_odd[i].astype(BF16)
        nxt = {}
        if layer + 1 < depth:
            w_in, outs = in_proj_weights(layer + 1)
            nxt = dict(next_gain=norm_g[layer + 1, 0], next_w_in=w_in, next_outs=outs)
        xf, *proj = layer_tail(y_a, y_b, xf, w_out, norm_g[layer, 1:4], w_up[layer].astype(BF16),
                               w_ffn_conv[layer], b_ffn_conv[layer], w_down[layer].astype(BF16),
                               seq, **nxt)
    return xf.reshape(batch, seq, d)
```

```python
import functools
import math

import numpy as np
import jax
import jax.numpy as jnp
from jax import lax
from jax.experimental import pallas as pl
from jax.experimental.pallas import tpu as pltpu

F32 = jnp.float32
BF16 = jnp.bfloat16

EPS = 1e-6
HEAD_DIM = 64
LANES = 128
SUBLANES = 8
HEADS_PER_TILE = LANES // HEAD_DIM
DA_PAIRS = ((128, 1), (512, 4), (2048, 16))
DA_SPAN = 128
BLOCK = 128
REL_BUCKETS = 32
REL_MAX_DIST = 2048
MASKED = -1e30
SB_EXIT = -105.0
VMEM_LIMIT = 56 * 1024 * 1024


def _cparams(*sem):
    return pltpu.CompilerParams(dimension_semantics=sem, vmem_limit_bytes=VMEM_LIMIT)


def _rms(x, g):
    return x * lax.rsqrt(jnp.mean(x * x, axis=-1, keepdims=True) + EPS) * g


def _norm_matmul_kernel(x_ref, g_ref, w_ref, *o_refs, col_chunk):
    h = _rms(x_ref[...], g_ref[...]).astype(BF16)
    col = 0
    for o_ref in o_refs:
        width = o_ref.shape[1]
        for c in range(0, width, col_chunk):
            o_ref[:, c:c + col_chunk] = jnp.dot(
                h, w_ref[:, col + c:col + c + col_chunk],
                preferred_element_type=F32).astype(o_ref.dtype)
        col += width


def norm_matmul(x, g, w, outs, tm=512, col_chunk=512):
    n, d = x.shape
    assert n % tm == 0 and sum(wd for wd, _ in outs) == w.shape[1]
    assert all(wd % col_chunk == 0 for wd, _ in outs)
    return pl.pallas_call(
        functools.partial(_norm_matmul_kernel, col_chunk=col_chunk),
        grid=(n // tm,),
        in_specs=[pl.BlockSpec((tm, d), lambda i: (i, 0)),
                  pl.BlockSpec((1, d), lambda i: (0, 0)),
                  pl.BlockSpec(w.shape, lambda i: (0, 0))],
        out_specs=[pl.BlockSpec((tm, wd), lambda i: (i, 0)) for wd, _ in outs],
        out_shape=[jax.ShapeDtypeStruct((n, wd), dt) for wd, dt in outs],
        compiler_params=_cparams("parallel"),
        name="norm_in_proj",
    )(x, g.reshape(1, d), w)


def _short_conv_kernel(gb_ref, gc_ref, xa_ref, w_ref, o_ref, pad_ref):
    ts = o_ref.shape[0]
    @pl.when(pl.program_id(1) == 0)
    def _():
        pad_ref[0:SUBLANES, :] = jnp.zeros((SUBLANES, pad_ref.shape[1]), F32)

    @pl.when(pl.program_id(1) != 0)
    def _():
        pad_ref[0:SUBLANES, :] = pad_ref[ts:ts + SUBLANES, :]

    c = gc_ref[...].astype(F32) * xa_ref[...].astype(F32)
    pad_ref[SUBLANES:, :] = c
    conv = (w_ref[2:3, :] * c
            + w_ref[1:2, :] * pad_ref[SUBLANES - 1:SUBLANES - 1 + ts, :]
            + w_ref[0:1, :] * pad_ref[SUBLANES - 2:SUBLANES - 2 + ts, :])
    o_ref[...] = (gb_ref[...].astype(F32) * conv).astype(o_ref.dtype)


def short_conv(pa, w_sc, batch, seq, ts=512):
    n = pa.shape[0]
    width = w_sc.shape[1]
    nt = seq // ts
    return pl.pallas_call(
        _short_conv_kernel,
        grid=(batch, nt),
        in_specs=[pl.BlockSpec((ts, width), lambda b, t: (b * nt + t, 0)),
                  pl.BlockSpec((ts, width), lambda b, t: (b * nt + t, 1)),
                  pl.BlockSpec((ts, width), lambda b, t: (b * nt + t, 2)),
                  pl.BlockSpec(w_sc.shape, lambda b, t: (0, 0))],
        out_specs=pl.BlockSpec((ts, width), lambda b, t: (b * nt + t, 0)),
        out_shape=jax.ShapeDtypeStruct((n, width), BF16),
        scratch_shapes=[pltpu.VMEM((ts + SUBLANES, width), F32)],
        compiler_params=_cparams("arbitrary", "arbitrary"),
        name="short_conv",
    )(pa, pa, pa, w_sc)


def _bucket_tables():
    rel = np.arange(BLOCK)[:, None] - np.arange(2 * BLOCK)[None, :] + DA_SPAN
    valid = (rel >= 0) & (rel <= DA_SPAN)
    max_exact = REL_BUCKETS // 2
    tabs = []
    for _, dil in DA_PAIRS:
        dist = np.clip(rel, 0, DA_SPAN) * dil
        d = np.maximum(dist, 1).astype(np.float32)
        large = max_exact + (np.log(d / np.float32(max_exact))
                             / np.float32(math.log(REL_MAX_DIST / max_exact))
                             * np.float32(REL_BUCKETS - max_exact)).astype(np.int32)
        large = np.minimum(large, REL_BUCKETS - 1)
        bucket = np.where(dist < max_exact, dist, large)
        tabs.append(np.where(valid, bucket, -1))
        tabs.append(np.where(valid & (np.arange(2 * BLOCK)[None, :] >= BLOCK), bucket, -1))
    return np.stack(tabs).astype(np.int32)


def _bias_table_kernel(rb_ref, idx_ref, o_ref):
    h = pl.program_id(1)
    idx = idx_ref[0]
    acc = jnp.full(idx.shape, MASKED, F32)
    for b in range(REL_BUCKETS):
        acc = jnp.where(idx == b, rb_ref[b, h], acc)
    o_ref[0, 0] = acc


def bias_tables(rel_bias):
    idx = jnp.asarray(_bucket_tables())
    heads = rel_bias.shape[1]
    return pl.pallas_call(
        _bias_table_kernel,
        grid=(idx.shape[0], heads),
        in_specs=[pl.BlockSpec(memory_space=pltpu.SMEM),
                  pl.BlockSpec((1, BLOCK, 2 * BLOCK), lambda g, h: (g, 0, 0))],
        out_specs=pl.BlockSpec((1, 1, BLOCK, 2 * BLOCK), lambda g, h: (g, h, 0, 0)),
        out_shape=jax.ShapeDtypeStruct((idx.shape[0], heads, BLOCK, 2 * BLOCK), F32),
        compiler_params=_cparams("arbitrary", "arbitrary"),
        name="bias_tables",
    )(rel_bias, idx)


def _dilated_attn_kernel(q_ref, k_ref, v_ref, *rest, group):
    dilated = [dil for _, dil in DA_PAIRS if dil > 1]
    hbm_refs = rest[:len(dilated)]
    bias_ref, o_ref, m_ref, l_ref, u_ref, s0_ref, s1_ref, grp_ref, sem = rest[len(dilated):]
    seq = q_ref.shape[0]
    width = hbm_refs[0].shape[-1] // 3
    b = pl.program_id(0)
    pair = pl.program_id(1)
    lane = lax.broadcasted_iota(jnp.int32, (1, LANES), 1)
    head_mask = (lane < HEAD_DIM, lane >= HEAD_DIM)
    nt_dims = (((1,), (1,)), ((), ()))
    ngroups = seq // BLOCK // group
    assert ngroups % 2 == 0

    def regroup_copy(n, a, r):
        rows = seq // dilated[n]
        cols = pl.ds(pl.multiple_of(a * width + pair * LANES, LANES), LANES)
        return pltpu.make_async_copy(hbm_refs[n].at[b, :, r, cols],
                                     grp_ref.at[n, a, pl.ds(r * rows, rows), :], sem.at[n, a])

    for n, dil in enumerate(dilated):
        for a in range(3):
            for r in range(dil):
                regroup_copy(n, a, r).start()

    for g, (_, dil) in enumerate(DA_PAIRS):
        nb = seq // dil // BLOCK
        if dil == 1:
            srcs = (q_ref, k_ref, v_ref)
        else:
            n = dilated.index(dil)
            for a in range(3):
                for r in range(dil):
                    regroup_copy(n, a, r).wait()
            srcs = tuple(grp_ref.at[n, a] for a in range(3))

        def block_rows(blk):
            r = blk // nb
            j = blk % nb
            own = pl.ds(pl.multiple_of(blk * BLOCK, BLOCK), BLOCK)
            prev = pl.ds(pl.multiple_of(jnp.maximum(blk - 1, 0) * BLOCK, BLOCK), BLOCK)
            out = own if dil == 1 else pl.ds(r + j * (BLOCK * dil), BLOCK, stride=dil)
            return own, prev, out, 2 * g + jnp.where(j == 0, 1, 0)

        def logits(grp, s_ref):
            for t in range(group):
                own, prev, _, table = block_rows(grp * group + t)
                q = srcs[0][own, :] * (1.0 / math.sqrt(HEAD_DIM))
                k2 = jnp.concatenate([srcs[1][prev, :], srcs[1][own, :]], axis=0).astype(BF16)
                for h in range(HEADS_PER_TILE):
                    qh = jnp.where(head_mask[h], q, 0.0).astype(BF16)
                    s = lax.dot_general(qh, k2, nt_dims, preferred_element_type=F32)
                    s_ref[HEADS_PER_TILE * t + h] = s + bias_ref[table, h]

        def finish(grp, s_ref):
            for t in range(group):
                own, prev, out, _ = block_rows(grp * group + t)
                v2 = jnp.concatenate([srcs[2][prev, :], srcs[2][own, :]], axis=0).astype(BF16)
                v2 = jnp.concatenate([v2, jnp.ones(v2.shape, BF16)], axis=1)
                stats = []
                for h in range(HEADS_PER_TILE):
                    s = s_ref[HEADS_PER_TILE * t + h]
                    m = jnp.max(s, axis=-1, keepdims=True)
                    p = jnp.exp(s - m)
                    pv = jnp.dot(p.astype(BF16), v2, preferred_element_type=F32)
                    stats.append((jnp.broadcast_to(m, (BLOCK, LANES)), pv[:, LANES:],
                                  pv[:, :LANES]))
                both = [jnp.where(head_mask[0], x0, x1) for x0, x1 in zip(*stats)]
                m_ref[g, out, :] = both[0]
                l_ref[g, out, :] = both[1]
                u_ref[g, out, :] = both[2]

        def body(i, carry):
            logits(2 * i + 1, s1_ref)
            finish(2 * i, s0_ref)
            logits(jnp.minimum(2 * i + 2, ngroups - 1), s0_ref)
            finish(2 * i + 1, s1_ref)
            return carry

        logits(jnp.int32(0), s0_ref)
        lax.fori_loop(0, ngroups // 2, body, 0)

    def merge(c, carry):
        rs = pl.ds(pl.multiple_of(c * BLOCK, BLOCK), BLOCK)
        m = m_ref[:, rs, :]
        wgt = jnp.exp(m - jnp.max(m, axis=0, keepdims=True))
        num = jnp.sum(wgt * u_ref[:, rs, :], axis=0)
        den = jnp.sum(wgt * l_ref[:, rs, :], axis=0)
        o_ref[rs, :] = (num / den).astype(o_ref.dtype)
        return carry

    lax.fori_loop(0, seq // BLOCK, merge, 0)


def dilated_attention(qkv, bias, batch, seq, group=4):
    n = qkv.shape[0]
    width = qkv.shape[1] // 3
    nhp = width // LANES
    dilated = [dil for _, dil in DA_PAIRS if dil > 1]
    blk = lambda off: pl.BlockSpec((seq, LANES), lambda b, p: (b, off + p))
    logits_buf = pltpu.VMEM((group * HEADS_PER_TILE, BLOCK, 2 * BLOCK), F32)
    by_residue = [qkv.reshape(batch, seq // dil, dil, qkv.shape[1]) for dil in dilated]
    return pl.pallas_call(
        functools.partial(_dilated_attn_kernel, group=group),
        grid=(batch, nhp),
        in_specs=[blk(0), blk(nhp), blk(2 * nhp)]
        + [pl.BlockSpec(memory_space=pl.ANY)] * len(dilated)
        + [pl.BlockSpec((bias.shape[0], HEADS_PER_TILE, BLOCK, 2 * BLOCK),
                        lambda b, p: (0, p, 0, 0))],
        out_specs=pl.BlockSpec((seq, LANES), lambda b, p: (b, p)),
        out_shape=jax.ShapeDtypeStruct((n, width), BF16),
        scratch_shapes=[pltpu.VMEM((len(DA_PAIRS), seq, LANES), F32)] * 3 + [logits_buf] * 2
        + [pltpu.VMEM((len(dilated), 3, seq, LANES), F32),
           pltpu.SemaphoreType.DMA((len(dilated), 3))],
        compiler_params=_cparams("parallel", "parallel"),
        name="dilated_attention",
    )(qkv, qkv, qkv, *by_residue, bias)


def _conformer_conv_kernel(a_ref, gate_ref, w_ref, b_ref, lg_ref, lb_ref, o_ref,
                           pad_ref, sh_ref, wb_ref, *, taps, halo, chunk):
    ts = o_ref.shape[0]

    @pl.when(pl.program_id(1) == 0)
    def _():
        pad_ref[0:halo, :] = jnp.zeros((halo, pad_ref.shape[1]), F32)

    @pl.when(pl.program_id(1) != 0)
    def _():
        pad_ref[0:halo, :] = pad_ref[ts:ts + halo, :]

    pad_ref[halo:, :] = a_ref[...].astype(F32) * jax.nn.sigmoid(gate_ref[...].astype(F32))

    lead = halo - (taps - 1)
    n_a = [len(range(b, taps, SUBLANES)) for b in range(SUBLANES)]
    width = o_ref.shape[1]
    for b in range(SUBLANES):
        groups = ts // SUBLANES + n_a[b] - 1
        win = pad_ref[lead + b:lead + b + groups * SUBLANES, :]
        sh_ref[b, 0:groups] = win.reshape(groups, SUBLANES, width)
    for j in range(taps):
        wb_ref[j] = jnp.broadcast_to(w_ref[j:j + 1, :], (SUBLANES, width))

    cg = chunk // SUBLANES
    for c0 in range(0, ts, chunk):
        acc = jnp.zeros((cg, SUBLANES, width), F32)
        for b in range(SUBLANES):
            for a in range(n_a[b]):
                g0 = c0 // SUBLANES + a
                acc = acc + wb_ref[SUBLANES * a + b][None] * sh_ref[b, g0:g0 + cg]
        y = acc.reshape(chunk, width) + b_ref[...]
        mu = jnp.mean(y, axis=-1, keepdims=True)
        yc = y - mu
        var = jnp.mean(yc * yc, axis=-1, keepdims=True)
        z = yc * lax.rsqrt(var + EPS) * lg_ref[...] + lb_ref[...]
        o_ref[c0:c0 + chunk, :] = (z * jax.nn.sigmoid(z)).astype(o_ref.dtype)


def conformer_conv(p, w_cc, b_cc, ln_g, ln_b, batch, seq, ts=256, chunk=32):
    n = p.shape[0]
    taps, width = w_cc.shape
    halo = -(-(taps - 1) // SUBLANES) * SUBLANES
    nt = seq // ts
    vec = lambda v: v.reshape(1, width)
    vspec = pl.BlockSpec((1, width), lambda b, t: (0, 0))
    return pl.pallas_call(
        functools.partial(_conformer_conv_kernel, taps=taps, halo=halo, chunk=chunk),
        grid=(batch, nt),
        in_specs=[pl.BlockSpec((ts, width), lambda b, t: (b * nt + t, 0)),
                  pl.BlockSpec((ts, width), lambda b, t: (b * nt + t, 1)),
                  pl.BlockSpec(w_cc.shape, lambda b, t: (0, 0)),
                  vspec, vspec, vspec],
        out_specs=pl.BlockSpec((ts, width), lambda b, t: (b * nt + t, 0)),
        out_shape=jax.ShapeDtypeStruct((n, width), BF16),
        scratch_shapes=[pltpu.VMEM((ts + halo, width), F32),
                        pltpu.VMEM((SUBLANES, (ts + halo) // SUBLANES, SUBLANES, width), F32),
                        pltpu.VMEM((taps, SUBLANES, width), F32)],
        compiler_params=_cparams("arbitrary", "arbitrary"),
        name="conformer_conv",
    )(p, p, w_cc, vec(b_cc), vec(ln_g), vec(ln_b))


def _suffix_sum_matrix():
    j = np.arange(BLOCK)[:, None]
    s = np.arange(BLOCK)[None, :]
    half = -np.concatenate([(j >= s).astype(np.float32), np.ones((BLOCK, BLOCK), np.float32)],
                           axis=1)
    return np.concatenate([half, half], axis=0)


def _stick_breaking_kernel(q_ref, k_ref, v_ref, u_ref, o_ref, qm_ref, run_ref, acc_ref, cat_ref):
    tq = q_ref.shape[0]
    nsub = tq // BLOCK
    qi = pl.program_id(2)
    lane = lax.broadcasted_iota(jnp.int32, (1, LANES), 1)
    head_mask = (lane < HEAD_DIM, lane >= HEAD_DIM)
    nt_dims = (((1,), (1,)), ((), ()))
    rows2 = HEADS_PER_TILE * BLOCK
    row = lax.broadcasted_iota(jnp.int32, (rows2, BLOCK), 0)
    col = lax.broadcasted_iota(jnp.int32, (rows2, BLOCK), 1)
    before = col < jnp.bitwise_and(row, BLOCK - 1)

    for r in range(nsub):
        q = q_ref[r * BLOCK:(r + 1) * BLOCK, :].astype(F32) * (1.0 / math.sqrt(HEAD_DIM))
        qm_ref[r] = jnp.concatenate(
            [jnp.where(head_mask[h], q, 0.0) for h in range(HEADS_PER_TILE)], axis=0).astype(BF16)
    run_ref[...] = jnp.zeros(run_ref.shape, F32)
    acc_ref[...] = jnp.zeros(acc_ref.shape, F32)

    def step(d, own_block):
        kbs = [qi * nsub + r - d for r in range(nsub)]
        kss = [pl.ds(pl.multiple_of(jnp.maximum(kb, 0) * BLOCK, BLOCK), BLOCK) for kb in kbs]
        zs = [lax.dot_general(qm_ref[r], k_ref[kss[r], :], nt_dims, preferred_element_type=F32)
              for r in range(nsub)]
        for r, z in enumerate(zs):
            neg_log_rest = jnp.maximum(z, 0.0) + jnp.log(1.0 + jnp.exp(-jnp.abs(z)))
            if own_block:
                neg_log_rest = jnp.where(before, neg_log_rest, 0.0)
            hi = neg_log_rest.astype(BF16)
            lo = (neg_log_rest - hi.astype(F32)).astype(BF16)
            cat_ref[r * rows2:(r + 1) * rows2, :] = jnp.concatenate([hi, lo], axis=1)
        sums = jnp.dot(cat_ref[...], u_ref[...], preferred_element_type=F32)
        weights = []
        for r in range(nsub):
            sub = sums[r * rows2:(r + 1) * rows2]
            run = run_ref[r]
            if not own_block:
                run = jnp.where(kbs[r] < 0, MASKED, run)
            a = jnp.exp(zs[r] + sub[:, 0:BLOCK] + run)
            if own_block:
                a = jnp.where(before, a, 0.0)
            weights.append(a.astype(BF16))
            run_ref[r] = run + sub[:, BLOCK:2 * BLOCK]
        for r in range(nsub):
            acc_ref[r] += jnp.dot(weights[r], v_ref[kss[r], :], preferred_element_type=F32)

    def still_open():
        return jnp.max(run_ref[...]) > SB_EXIT

    def body(c):
        d, _ = c
        step(d, False)
        return d + 1, still_open()

    step(0, True)
    lax.while_loop(lambda c: c[1], body, (jnp.int32(1), still_open()))
    for r in range(nsub):
        o_ref[r * BLOCK:(r + 1) * BLOCK, :] = jnp.where(
            head_mask[0], acc_ref[r, 0:BLOCK, :], acc_ref[r, BLOCK:rows2, :]).astype(o_ref.dtype)


def stick_breaking(p, col0, batch, seq, tq=1024):
    n = p.shape[0]
    width = (p.shape[1] - col0) // 3
    nhp = width // LANES
    c0 = col0 // LANES
    nq = seq // tq
    u = jnp.asarray(_suffix_sum_matrix(), BF16)
    return pl.pallas_call(
        _stick_breaking_kernel,
        grid=(batch, nhp, nq),
        in_specs=[pl.BlockSpec((tq, LANES), lambda b, p_, i: (b * nq + i, c0 + p_)),
                  pl.BlockSpec((seq, LANES), lambda b, p_, i: (b, c0 + nhp + p_)),
                  pl.BlockSpec((seq, LANES), lambda b, p_, i: (b, c0 + 2 * nhp + p_)),
                  pl.BlockSpec(u.shape, lambda b, p_, i: (0, 0))],
        out_specs=pl.BlockSpec((tq, LANES), lambda b, p_, i: (b * nq + i, p_)),
        out_shape=jax.ShapeDtypeStruct((n, width), BF16),
        scratch_shapes=[pltpu.VMEM((tq // BLOCK, HEADS_PER_TILE * BLOCK, LANES), BF16),
                        pltpu.VMEM((tq // BLOCK, HEADS_PER_TILE * BLOCK, LANES), F32),
                        pltpu.VMEM((tq // BLOCK, HEADS_PER_TILE * BLOCK, LANES), F32),
                        pltpu.VMEM((HEADS_PER_TILE * tq, 2 * BLOCK), BF16)],
        compiler_params=_cparams("parallel", "parallel", "arbitrary"),
        name="stick_breaking",
    )(p, p, p, u)


def _layer_tail_kernel(ya_ref, yb_ref, x_ref, wo_ref, g_ref, wup_ref, wc_ref, bc_ref, wd_ref,
                       *rest, tiles_per_seq, fc, col_chunk, has_next):
    if has_next:
        gn_ref, win_ref, o_ref, *p_refs, act_ref, pad_ref, carry_ref = rest
    else:
        o_ref, act_ref, pad_ref, carry_ref = rest
        p_refs = []
    tm = x_ref.shape[0]
    d_ff = wd_ref.shape[0]
    wa_rows = ya_ref.shape[1]

    mix = (jnp.dot(ya_ref[...], wo_ref[0:wa_rows, :], preferred_element_type=F32)
           + jnp.dot(yb_ref[...], wo_ref[wa_rows:, :], preferred_element_type=F32))
    x = x_ref[...] + _rms(mix, g_ref[0:1, :])
    h = _rms(x, g_ref[1:2, :]).astype(BF16)

    @pl.when(pl.program_id(0) % tiles_per_seq == 0)
    def _():
        carry_ref[...] = jnp.zeros(carry_ref.shape, F32)

    for n, c0 in enumerate(range(0, d_ff, fc)):
        cols = slice(c0, c0 + fc)
        gate = jnp.dot(h, wup_ref[:, cols], preferred_element_type=F32)
        up = jnp.dot(h, wup_ref[:, d_ff + c0:d_ff + c0 + fc], preferred_element_type=F32)
        pad = pad_ref.at[n % pad_ref.shape[0]]
        pad[0:SUBLANES, :] = carry_ref[:, cols]
        pad[SUBLANES:, :] = gate
        carry_ref[:, cols] = gate[tm - SUBLANES:, :]
        conv = (wc_ref[2:3, cols] * gate
                + wc_ref[1:2, cols] * pad[SUBLANES - 1:SUBLANES - 1 + tm, :]
                + wc_ref[0:1, cols] * pad[SUBLANES - 2:SUBLANES - 2 + tm, :]
                + bc_ref[:, cols])
        act_ref[:, cols] = (conv * jax.nn.sigmoid(conv) * up).astype(BF16)
    out = jnp.dot(act_ref[...], wd_ref[...], preferred_element_type=F32)
    x = x + _rms(out, g_ref[2:3, :])
    o_ref[...] = x

    if has_next:
        hn = _rms(x, gn_ref[...]).astype(BF16)
        col = 0
        for p_ref in p_refs:
            width = p_ref.shape[1]
            for c in range(0, width, col_chunk):
                p_ref[:, c:c + col_chunk] = jnp.dot(
                    hn, win_ref[:, col + c:col + c + col_chunk],
                    preferred_element_type=F32).astype(p_ref.dtype)
            col += width


def layer_tail(ya, yb, x, w_out, gains, w_up, w_conv, b_conv, w_down, seq,
               next_gain=None, next_w_in=None, next_outs=(), tm=512, fc=256, col_chunk=512):
    n, d = x.shape
    d_ff = w_down.shape[0]
    has_next = next_w_in is not None
    assert d_ff % fc == 0 and seq % tm == 0
    const = lambda shape: pl.BlockSpec(shape, lambda i: (0, 0), pipeline_mode=pl.Buffered(1))
    row = lambda width: pl.BlockSpec((tm, width), lambda i: (i, 0))
    in_specs = [row(ya.shape[1]), row(yb.shape[1]), row(d), const(w_out.shape), const((3, d)),
                const(w_up.shape), const(w_conv.shape), const((1, d_ff)), const(w_down.shape)]
    args = [ya, yb, x, w_out, gains, w_up, w_conv, b_conv.reshape(1, d_ff), w_down]
    out_specs = [row(d)]
    out_shape = [jax.ShapeDtypeStruct((n, d), F32)]
    if has_next:
        in_specs += [const((1, d)), const(next_w_in.shape)]
        args += [next_gain.reshape(1, d), next_w_in]
        out_specs += [row(wd) for wd, _ in next_outs]
        out_shape += [jax.ShapeDtypeStruct((n, wd), dt) for wd, dt in next_outs]
    return pl.pallas_call(
        functools.partial(_layer_tail_kernel, tiles_per_seq=seq // tm, fc=fc,
                          col_chunk=col_chunk, has_next=has_next),
        grid=(n // tm,),
        in_specs=in_specs,
        out_specs=out_specs,
        out_shape=out_shape,
        scratch_shapes=[pltpu.VMEM((tm, d_ff), BF16),
                        pltpu.VMEM((2, tm + SUBLANES, fc), F32),
                        pltpu.VMEM((SUBLANES, d_ff), F32)],
        compiler_params=_cparams("arbitrary"),
        name="layer_tail",
    )(*args)


def kernel(x, norm_g, rel_bias, w_in_even, w_out_even, w_sc, w_in_odd, w_out_odd, w_cc, b_cc,
           ln_cc_g, ln_cc_b, w_up, w_ffn_conv, b_ffn_conv, w_down):
    batch, seq, d = x.shape
    depth = norm_g.shape[0]
    sc_width = w_sc.shape[2]
    cc_width = w_cc.shape[2]
    xf = x.reshape(batch * seq, d)
    bias = bias_tables(rel_bias)

    def in_proj_weights(layer):
        if layer % 2 == 0:
            w_in = w_in_even[layer // 2].astype(BF16)
            conv_cols = 3 * sc_width
            return w_in, [(conv_cols, BF16), (w_in.shape[1] - conv_cols, F32)]
        w_in = w_in_odd[layer // 2].astype(BF16)
        return w_in, [(w_in.shape[1], BF16)]

    w_in, outs = in_proj_weights(0)
    proj = norm_matmul(xf, norm_g[0, 0], w_in, outs)
    for layer in range(depth):
        i = layer // 2
        if layer % 2 == 0:
            pa, qkv = proj
            y_a = short_conv(pa, w_sc[i], batch, seq)
            y_b = dilated_attention(qkv, bias, batch, seq)
            w_out = w_out_even[i].astype(BF16)
        else:
            (p,) = proj
            y_a = conformer_conv(p, w_cc[i], b_cc[i], ln_cc_g[i], ln_cc_b[i], batch, seq)
            y_b = stick_breaking(p, 2 * cc_width, batch, seq)
            w_out = w_out_odd[i].astype(BF16)
        nxt = {}
        if layer + 1 < depth:
            w_in, outs = in_proj_weights(layer + 1)
            nxt = dict(next_gain=norm_g[layer + 1, 0], next_w_in=w_in, next_outs=outs)
        xf, *proj = layer_tail(y_a, y_b, xf, w_out, norm_g[layer, 1:4], w_up[layer].astype(BF16),
                               w_ffn_conv[layer], b_ffn_conv[layer], w_down[layer].astype(BF16),
                               seq, **nxt)
    return xf.reshape(batch, seq, d)
```

```python
import functools
import math

import numpy as np
import jax
import jax.numpy as jnp
from jax import lax
from jax.experimental import pallas as pl
from jax.experimental.pallas import tpu as pltpu

F32 = jnp.float32
BF16 = jnp.bfloat16

EPS = 1e-6
HEAD_DIM = 64
LANES = 128
SUBLANES = 8
HEADS_PER_TILE = LANES // HEAD_DIM
DA_PAIRS = ((128, 1), (512, 4), (2048, 16))
DA_SPAN = 128
MERGE_DIL = 4
BLOCK = 128
REL_BUCKETS = 32
REL_MAX_DIST = 2048
MASKED = -1e30
SB_EXIT = -105.0
VMEM_LIMIT = 56 * 1024 * 1024


def _cparams(*sem):
    return pltpu.CompilerParams(dimension_semantics=sem, vmem_limit_bytes=VMEM_LIMIT)


def _rms(x, g):
    return x * lax.rsqrt(jnp.mean(x * x, axis=-1, keepdims=True) + EPS) * g


def _norm_matmul_kernel(x_ref, g_ref, w_ref, *o_refs, col_chunk):
    h = _rms(x_ref[...], g_ref[...]).astype(BF16)
    col = 0
    for o_ref in o_refs:
        width = o_ref.shape[1]
        for c in range(0, width, col_chunk):
            o_ref[:, c:c + col_chunk] = jnp.dot(
                h, w_ref[:, col + c:col + c + col_chunk],
                preferred_element_type=F32).astype(o_ref.dtype)
        col += width


def norm_matmul(x, g, w, outs, tm=512, col_chunk=512):
    n, d = x.shape
    assert n % tm == 0 and sum(wd for wd, _ in outs) == w.shape[1]
    assert all(wd % col_chunk == 0 for wd, _ in outs)
    return pl.pallas_call(
        functools.partial(_norm_matmul_kernel, col_chunk=col_chunk),
        grid=(n // tm,),
        in_specs=[pl.BlockSpec((tm, d), lambda i: (i, 0)),
                  pl.BlockSpec((1, d), lambda i: (0, 0)),
                  pl.BlockSpec(w.shape, lambda i: (0, 0))],
        out_specs=[pl.BlockSpec((tm, wd), lambda i: (i, 0)) for wd, _ in outs],
        out_shape=[jax.ShapeDtypeStruct((n, wd), dt) for wd, dt in outs],
        compiler_params=_cparams("parallel"),
        name="norm_in_proj",
    )(x, g.reshape(1, d), w)


def _short_conv_kernel(gb_ref, gc_ref, xa_ref, w_ref, o_ref, pad_ref):
    ts = o_ref.shape[0]
    @pl.when(pl.program_id(1) == 0)
    def _():
        pad_ref[0:SUBLANES, :] = jnp.zeros((SUBLANES, pad_ref.shape[1]), F32)

    @pl.when(pl.program_id(1) != 0)
    def _():
        pad_ref[0:SUBLANES, :] = pad_ref[ts:ts + SUBLANES, :]

    c = gc_ref[...].astype(F32) * xa_ref[...].astype(F32)
    pad_ref[SUBLANES:, :] = c
    conv = (w_ref[2:3, :] * c
            + w_ref[1:2, :] * pad_ref[SUBLANES - 1:SUBLANES - 1 + ts, :]
            + w_ref[0:1, :] * pad_ref[SUBLANES - 2:SUBLANES - 2 + ts, :])
    o_ref[...] = (gb_ref[...].astype(F32) * conv).astype(o_ref.dtype)


def short_conv(pa, w_sc, batch, seq, ts=512):
    n = pa.shape[0]
    width = w_sc.shape[1]
    nt = seq // ts
    return pl.pallas_call(
        _short_conv_kernel,
        grid=(batch, nt),
        in_specs=[pl.BlockSpec((ts, width), lambda b, t: (b * nt + t, 0)),
                  pl.BlockSpec((ts, width), lambda b, t: (b * nt + t, 1)),
                  pl.BlockSpec((ts, width), lambda b, t: (b * nt + t, 2)),
                  pl.BlockSpec(w_sc.shape, lambda b, t: (0, 0))],
        out_specs=pl.BlockSpec((ts, width), lambda b, t: (b * nt + t, 0)),
        out_shape=jax.ShapeDtypeStruct((n, width), BF16),
        scratch_shapes=[pltpu.VMEM((ts + SUBLANES, width), F32)],
        compiler_params=_cparams("arbitrary", "arbitrary"),
        name="short_conv",
    )(pa, pa, pa, w_sc)


def _bucket_tables():
    rel = np.arange(BLOCK)[:, None] - np.arange(2 * BLOCK)[None, :] + DA_SPAN
    valid = (rel >= 0) & (rel <= DA_SPAN)
    max_exact = REL_BUCKETS // 2
    tabs = []
    for _, dil in DA_PAIRS:
        dist = np.clip(rel, 0, DA_SPAN) * dil
        d = np.maximum(dist, 1).astype(np.float32)
        large = max_exact + (np.log(d / np.float32(max_exact))
                             / np.float32(math.log(REL_MAX_DIST / max_exact))
                             * np.float32(REL_BUCKETS - max_exact)).astype(np.int32)
        large = np.minimum(large, REL_BUCKETS - 1)
        bucket = np.where(dist < max_exact, dist, large)
        tabs.append(np.where(valid, bucket, -1))
        tabs.append(np.where(valid & (np.arange(2 * BLOCK)[None, :] >= BLOCK), bucket, -1))
    return np.stack(tabs).astype(np.int32)


def _bias_table_kernel(rb_ref, idx_ref, o_ref):
    h = pl.program_id(1)
    idx = idx_ref[0]
    acc = jnp.full(idx.shape, MASKED, F32)
    for b in range(REL_BUCKETS):
        acc = jnp.where(idx == b, rb_ref[b, h], acc)
    o_ref[0, 0] = acc


def bias_tables(rel_bias):
    idx = jnp.asarray(_bucket_tables())
    heads = rel_bias.shape[1]
    return pl.pallas_call(
        _bias_table_kernel,
        grid=(idx.shape[0], heads),
        in_specs=[pl.BlockSpec(memory_space=pltpu.SMEM),
                  pl.BlockSpec((1, BLOCK, 2 * BLOCK), lambda g, h: (g, 0, 0))],
        out_specs=pl.BlockSpec((1, 1, BLOCK, 2 * BLOCK), lambda g, h: (g, h, 0, 0)),
        out_shape=jax.ShapeDtypeStruct((idx.shape[0], heads, BLOCK, 2 * BLOCK), F32),
        compiler_params=_cparams("arbitrary", "arbitrary"),
        name="bias_tables",
    )(rel_bias, idx)


def _regrouped_dilations():
    return [dil for _, dil in DA_PAIRS if dil > 1 and dil % SUBLANES == 0]


def _dilated_attn_kernel(q_ref, k_ref, v_ref, *rest, group):
    dilated = _regrouped_dilations()
    hbm_refs = rest[:len(dilated)]
    bias_ref, o_ref, m_ref, l_ref, u_ref, s0_ref, s1_ref, grp_ref, y_ref, sem = rest[len(dilated):]
    seq = q_ref.shape[0]
    width = hbm_refs[0].shape[-1] // 3
    b = pl.program_id(0)
    pair = pl.program_id(1)
    lane = lax.broadcasted_iota(jnp.int32, (1, LANES), 1)
    head_mask = (lane < HEAD_DIM, lane >= HEAD_DIM)
    nt_dims = (((1,), (1,)), ((), ()))
    ngroups = seq // BLOCK // group
    assert ngroups % 2 == 0

    def regroup_copy(n, a, r):
        rows = seq // dilated[n]
        cols = pl.ds(pl.multiple_of(a * width + pair * LANES, LANES), LANES)
        return pltpu.make_async_copy(hbm_refs[n].at[b, :, r, cols],
                                     grp_ref.at[n, a, pl.ds(r * rows, rows), :], sem.at[n, a])

    for n, dil in enumerate(dilated):
        for a in range(3):
            for r in range(dil):
                regroup_copy(n, a, r).start()

    for g, (_, dil) in enumerate(DA_PAIRS):
        nb = seq // dil // BLOCK
        srcs = (q_ref, k_ref, v_ref)
        if dil in dilated:
            n = dilated.index(dil)
            for a in range(3):
                for r in range(dil):
                    regroup_copy(n, a, r).wait()
            srcs = tuple(grp_ref.at[n, a] for a in range(3))

        def block_rows(blk):
            r = blk // nb
            j = blk % nb
            seq_rows = lambda jj: pl.ds(r + jj * (BLOCK * dil), BLOCK, stride=dil)
            own = pl.ds(pl.multiple_of(blk * BLOCK, BLOCK), BLOCK)
            prev = pl.ds(pl.multiple_of(jnp.maximum(blk - 1, 0) * BLOCK, BLOCK), BLOCK)
            if dil <= MERGE_DIL:
                out = own
            else:
                sub = dil // MERGE_DIL
                out = pl.ds((r % MERGE_DIL) * (seq // MERGE_DIL) + sub * BLOCK * j
                            + r // MERGE_DIL, BLOCK, stride=sub)
            if dil > 1 and dil not in dilated:
                own, prev = seq_rows(j), seq_rows(jnp.maximum(j - 1, 0))
            return own, prev, out, 2 * g + jnp.where(j == 0, 1, 0)

        def logits(grp, s_ref):
            for t in range(group):
                own, prev, _, table = block_rows(grp * group + t)
                q = srcs[0][own, :] * (1.0 / math.sqrt(HEAD_DIM))
                k2 = jnp.concatenate([srcs[1][prev, :], srcs[1][own, :]], axis=0).astype(BF16)
                for h in range(HEADS_PER_TILE):
                    qh = jnp.where(head_mask[h], q, 0.0).astype(BF16)
                    s = lax.dot_general(qh, k2, nt_dims, preferred_element_type=F32)
                    s_ref[HEADS_PER_TILE * t + h] = s + bias_ref[table, h]

        def finish(grp, s_ref):
            for t in range(group):
                own, prev, out, _ = block_rows(grp * group + t)
                v2 = jnp.concatenate([srcs[2][prev, :], srcs[2][own, :]], axis=0).astype(BF16)
                v2 = jnp.concatenate([v2, jnp.ones(v2.shape, BF16)], axis=1)
                stats = []
                for h in range(HEADS_PER_TILE):
                    s = s_ref[HEADS_PER_TILE * t + h]
                    m = jnp.max(s, axis=-1, keepdims=True)
                    p = jnp.exp(s - m)
                    pv = jnp.dot(p.astype(BF16), v2, preferred_element_type=F32)
                    stats.append((jnp.broadcast_to(m, (BLOCK, LANES)), pv[:, LANES:],
                                  pv[:, :LANES]))
                both = [jnp.where(head_mask[0], x0, x1) for x0, x1 in zip(*stats)]
                m_ref[g, out, :] = both[0]
                l_ref[g, out, :] = both[1]
                u_ref[g, out, :] = both[2]

        def body(i, carry):
            logits(2 * i + 1, s1_ref)
            finish(2 * i, s0_ref)
            logits(jnp.minimum(2 * i + 2, ngroups - 1), s0_ref)
            finish(2 * i + 1, s1_ref)
            return carry

        logits(jnp.int32(0), s0_ref)
        lax.fori_loop(0, ngroups // 2, body, 0)

    nbm = seq // MERGE_DIL // BLOCK

    def merge(c, carry):
        own = pl.ds(pl.multiple_of(c * BLOCK, BLOCK), BLOCK)
        in_seq = pl.ds(c // nbm + (c % nbm) * (BLOCK * MERGE_DIL), BLOCK, stride=MERGE_DIL)
        rows = [in_seq] + [own] * (len(DA_PAIRS) - 1)
        ms = [m_ref[g, rows[g], :] for g in range(len(DA_PAIRS))]
        top = functools.reduce(jnp.maximum, ms)
        wgts = [jnp.exp(m - top) for m in ms]
        num = sum(w * u_ref[g, rows[g], :] for g, w in enumerate(wgts))
        den = sum(w * l_ref[g, rows[g], :] for g, w in enumerate(wgts))
        y_ref[in_seq, :] = num / den
        return carry

    lax.fori_loop(0, seq // BLOCK, merge, 0)

    def emit(c, carry):
        rs = pl.ds(pl.multiple_of(c * BLOCK, BLOCK), BLOCK)
        o_ref[rs, :] = y_ref[rs, :].astype(o_ref.dtype)
        return carry

    lax.fori_loop(0, seq // BLOCK, emit, 0)


def dilated_attention(qkv, bias, batch, seq, group=4):
    n = qkv.shape[0]
    width = qkv.shape[1] // 3
    nhp = width // LANES
    dilated = _regrouped_dilations()
    blk = lambda off: pl.BlockSpec((seq, LANES), lambda b, p: (b, off + p))
    logits_buf = pltpu.VMEM((group * HEADS_PER_TILE, BLOCK, 2 * BLOCK), F32)
    by_residue = [qkv.reshape(batch, seq // dil, dil, qkv.shape[1]) for dil in dilated]
    return pl.pallas_call(
        functools.partial(_dilated_attn_kernel, group=group),
        grid=(batch, nhp),
        in_specs=[blk(0), blk(nhp), blk(2 * nhp)]
        + [pl.BlockSpec(memory_space=pl.ANY)] * len(dilated)
        + [pl.BlockSpec((bias.shape[0], HEADS_PER_TILE, BLOCK, 2 * BLOCK),
                        lambda b, p: (0, p, 0, 0))],
        out_specs=pl.BlockSpec((seq, LANES), lambda b, p: (b, p)),
        out_shape=jax.ShapeDtypeStruct((n, width), BF16),
        scratch_shapes=[pltpu.VMEM((len(DA_PAIRS), seq, LANES), F32)] * 3 + [logits_buf] * 2
        + [pltpu.VMEM((len(dilated), 3, seq, LANES), F32),
           pltpu.VMEM((seq, LANES), F32),
           pltpu.SemaphoreType.DMA((len(dilated), 3))],
        compiler_params=_cparams("parallel", "parallel"),
        name="dilated_attention",
    )(qkv, qkv, qkv, *by_residue, bias)


def _conformer_conv_kernel(a_ref, gate_ref, w_ref, b_ref, lg_ref, lb_ref, o_ref,
                           pad_ref, sh_ref, wb_ref, *, taps, halo, chunk):
    ts = o_ref.shape[0]

    @pl.when(pl.program_id(1) == 0)
    def _():
        pad_ref[0:halo, :] = jnp.zeros((halo, pad_ref.shape[1]), F32)

    @pl.when(pl.program_id(1) != 0)
    def _():
        pad_ref[0:halo, :] = pad_ref[ts:ts + halo, :]

    pad_ref[halo:, :] = a_ref[...].astype(F32) * jax.nn.sigmoid(gate_ref[...].astype(F32))

    lead = halo - (taps - 1)
    n_a = [len(range(b, taps, SUBLANES)) for b in range(SUBLANES)]
    width = o_ref.shape[1]
    for b in range(SUBLANES):
        groups = ts // SUBLANES + n_a[b] - 1
        win = pad_ref[lead + b:lead + b + groups * SUBLANES, :]
        sh_ref[b, 0:groups] = win.reshape(groups, SUBLANES, width)
    for j in range(taps):
        wb_ref[j] = jnp.broadcast_to(w_ref[j:j + 1, :], (SUBLANES, width))

    cg = chunk // SUBLANES
    for c0 in range(0, ts, chunk):
        acc = jnp.zeros((cg, SUBLANES, width), F32)
        for b in range(SUBLANES):
            for a in range(n_a[b]):
                g0 = c0 // SUBLANES + a
                acc = acc + wb_ref[SUBLANES * a + b][None] * sh_ref[b, g0:g0 + cg]
        y = acc.reshape(chunk, width) + b_ref[...]
        mu = jnp.mean(y, axis=-1, keepdims=True)
        yc = y - mu
        var = jnp.mean(yc * yc, axis=-1, keepdims=True)
        z = yc * lax.rsqrt(var + EPS) * lg_ref[...] + lb_ref[...]
        o_ref[c0:c0 + chunk, :] = (z * jax.nn.sigmoid(z)).astype(o_ref.dtype)


def conformer_conv(p, w_cc, b_cc, ln_g, ln_b, batch, seq, ts=256, chunk=32):
    n = p.shape[0]
    taps, width = w_cc.shape
    halo = -(-(taps - 1) // SUBLANES) * SUBLANES
    nt = seq // ts
    vec = lambda v: v.reshape(1, width)
    vspec = pl.BlockSpec((1, width), lambda b, t: (0, 0))
    return pl.pallas_call(
        functools.partial(_conformer_conv_kernel, taps=taps, halo=halo, chunk=chunk),
        grid=(batch, nt),
        in_specs=[pl.BlockSpec((ts, width), lambda b, t: (b * nt + t, 0)),
                  pl.BlockSpec((ts, width), lambda b, t: (b * nt + t, 1)),
                  pl.BlockSpec(w_cc.shape, lambda b, t: (0, 0)),
                  vspec, vspec, vspec],
        out_specs=pl.BlockSpec((ts, width), lambda b, t: (b * nt + t, 0)),
        out_shape=jax.ShapeDtypeStruct((n, width), BF16),
        scratch_shapes=[pltpu.VMEM((ts + halo, width), F32),
                        pltpu.VMEM((SUBLANES, (ts + halo) // SUBLANES, SUBLANES, width), F32),
                        pltpu.VMEM((taps, SUBLANES, width), F32)],
        compiler_params=_cparams("arbitrary", "arbitrary"),
        name="conformer_conv",
    )(p, p, w_cc, vec(b_cc), vec(ln_g), vec(ln_b))


def _suffix_sum_matrix():
    j = np.arange(BLOCK)[:, None]
    s = np.arange(BLOCK)[None, :]
    half = -np.concatenate([(j >= s).astype(np.float32), np.ones((BLOCK, BLOCK), np.float32)],
                           axis=1)
    return np.concatenate([half, half], axis=0)


def _stick_breaking_kernel(q_ref, k_ref, v_ref, u_ref, o_ref, qm_ref, run_ref, acc_ref, cat_ref,
                           *, fixed_steps):
    tq = q_ref.shape[0]
    nsub = tq // BLOCK
    qi = pl.program_id(2)
    lane = lax.broadcasted_iota(jnp.int32, (1, LANES), 1)
    head_mask = (lane < HEAD_DIM, lane >= HEAD_DIM)
    nt_dims = (((1,), (1,)), ((), ()))
    rows2 = HEADS_PER_TILE * BLOCK
    row = lax.broadcasted_iota(jnp.int32, (rows2, BLOCK), 0)
    col = lax.broadcasted_iota(jnp.int32, (rows2, BLOCK), 1)
    before = col < jnp.bitwise_and(row, BLOCK - 1)

    for r in range(nsub):
        q = q_ref[r * BLOCK:(r + 1) * BLOCK, :].astype(F32) * (1.0 / math.sqrt(HEAD_DIM))
        qm_ref[r] = jnp.concatenate(
            [jnp.where(head_mask[h], q, 0.0) for h in range(HEADS_PER_TILE)], axis=0).astype(BF16)
    run_ref[...] = jnp.zeros(run_ref.shape, F32)

    sign_bit = jnp.int32(-2 ** 31)

    def sweep(steps, first):
        plan = []
        for slot, d in enumerate(steps):
            kbs = [qi * nsub + r - d for r in range(nsub)]
            kss = [pl.ds(pl.multiple_of(jnp.maximum(kb, 0) * BLOCK, BLOCK), BLOCK) for kb in kbs]
            zs = [lax.dot_general(qm_ref[r], k_ref[kss[r], :], nt_dims,
                                  preferred_element_type=F32) for r in range(nsub)]
            plan.append((isinstance(d, int) and d == 0, kbs, kss, zs, cat_ref.at[slot]))
        all_sums = []
        for own_block, _, _, zs, cat in plan:
            for r, z in enumerate(zs):
                neg_abs = pltpu.bitcast(pltpu.bitcast(z, jnp.int32) | sign_bit, F32)
                neg_log_rest = jnp.maximum(z, 0.0) + jnp.log(1.0 + jnp.exp(neg_abs))
                if own_block:
                    neg_log_rest = jnp.where(before, neg_log_rest, 0.0)
                hi = neg_log_rest.astype(BF16)
                lo = (neg_log_rest - hi.astype(F32)).astype(BF16)
                cat[r * rows2:(r + 1) * rows2, :] = jnp.concatenate([hi, lo], axis=1)
            all_sums.append(jnp.dot(cat[...], u_ref[...], preferred_element_type=F32))
        weights = [[] for _ in range(nsub)]
        values = [[] for _ in range(nsub)]
        for (own_block, kbs, kss, zs, _), sums in zip(plan, all_sums):
            for r in range(nsub):
                sub = sums[r * rows2:(r + 1) * rows2]
                run = run_ref[r]
                a = jnp.exp(zs[r] + sub[:, 0:BLOCK] + run)
                if own_block:
                    a = jnp.where(before, a, 0.0)
                weights[r].append(a.astype(BF16))
                run_ref[r] = run + sub[:, BLOCK:2 * BLOCK]
                v_blk = v_ref[kss[r], :]
                values[r].append(v_blk if own_block else
                                 jnp.where(kbs[r] < 0, jnp.zeros_like(v_blk), v_blk))
        for r in range(nsub):
            out = jnp.dot(jnp.concatenate(weights[r], axis=1), jnp.concatenate(values[r], axis=0),
                          preferred_element_type=F32)
            acc_ref[r] = out if first else acc_ref[r] + out

    def still_open(d):
        return jnp.logical_and(d < (qi + 1) * nsub, jnp.max(run_ref[...]) > SB_EXIT)

    def body(c):
        d, _ = c
        sweep([d], False)
        return d + 1, still_open(d + 1)

    sweep(list(range(fixed_steps)), True)
    lax.while_loop(lambda c: c[1], body, (jnp.int32(fixed_steps), still_open(fixed_steps)))
    for r in range(nsub):
        o_ref[r * BLOCK:(r + 1) * BLOCK, :] = jnp.where(
            head_mask[0], acc_ref[r, 0:BLOCK, :], acc_ref[r, BLOCK:rows2, :]).astype(o_ref.dtype)


def stick_breaking(p, col0, batch, seq, tq=1024, fixed_steps=3):
    n = p.shape[0]
    width = (p.shape[1] - col0) // 3
    nhp = width // LANES
    c0 = col0 // LANES
    nq = seq // tq
    u = jnp.asarray(_suffix_sum_matrix(), BF16)
    return pl.pallas_call(
        functools.partial(_stick_breaking_kernel, fixed_steps=fixed_steps),
        grid=(batch, nhp, nq),
        in_specs=[pl.BlockSpec((tq, LANES), lambda b, p_, i: (b * nq + i, c0 + p_)),
                  pl.BlockSpec((seq, LANES), lambda b, p_, i: (b, c0 + nhp + p_)),
                  pl.BlockSpec((seq, LANES), lambda b, p_, i: (b, c0 + 2 * nhp + p_)),
                  pl.BlockSpec(u.shape, lambda b, p_, i: (0, 0))],
        out_specs=pl.BlockSpec((tq, LANES), lambda b, p_, i: (b * nq + i, p_)),
        out_shape=jax.ShapeDtypeStruct((n, width), BF16),
        scratch_shapes=[pltpu.VMEM((tq // BLOCK, HEADS_PER_TILE * BLOCK, LANES), BF16),
                        pltpu.VMEM((tq // BLOCK, HEADS_PER_TILE * BLOCK, LANES), F32),
                        pltpu.VMEM((tq // BLOCK, HEADS_PER_TILE * BLOCK, LANES), F32),
                        pltpu.VMEM((fixed_steps, HEADS_PER_TILE * tq, 2 * BLOCK), BF16)],
        compiler_params=_cparams("parallel", "parallel", "arbitrary"),
        name="stick_breaking",
    )(p, p, p, u)


def _layer_tail_kernel(ya_ref, yb_ref, x_ref, wo_ref, g_ref, wup_ref, wc_ref, bc_ref, wd_ref,
                       *rest, tiles_per_seq, fc, col_chunk, has_next):
    if has_next:
        gn_ref, win_ref, o_ref, *p_refs, act_ref, pad_ref, carry_ref = rest
    else:
        o_ref, act_ref, pad_ref, carry_ref = rest
        p_refs = []
    tm = x_ref.shape[0]
    d_ff = wd_ref.shape[0]
    wa_rows = ya_ref.shape[1]

    mix = (jnp.dot(ya_ref[...], wo_ref[0:wa_rows, :], preferred_element_type=F32)
           + jnp.dot(yb_ref[...], wo_ref[wa_rows:, :], preferred_element_type=F32))
    x = x_ref[...] + _rms(mix, g_ref[0:1, :])
    h = _rms(x, g_ref[1:2, :]).astype(BF16)

    @pl.when(pl.program_id(0) % tiles_per_seq == 0)
    def _():
        carry_ref[...] = jnp.zeros(carry_ref.shape, F32)

    for n, c0 in enumerate(range(0, d_ff, fc)):
        cols = slice(c0, c0 + fc)
        gate = jnp.dot(h, wup_ref[:, cols], preferred_element_type=F32)
        up = jnp.dot(h, wup_ref[:, d_ff + c0:d_ff + c0 + fc], preferred_element_type=F32)
        pad = pad_ref.at[n % pad_ref.shape[0]]
        pad[0:SUBLANES, :] = carry_ref[:, cols]
        pad[SUBLANES:, :] = gate
        carry_ref[:, cols] = gate[tm - SUBLANES:, :]
        conv = (wc_ref[2:3, cols] * gate
                + wc_ref[1:2, cols] * pad[SUBLANES - 1:SUBLANES - 1 + tm, :]
                + wc_ref[0:1, cols] * pad[SUBLANES - 2:SUBLANES - 2 + tm, :]
                + bc_ref[:, cols])
        act_ref[:, cols] = (conv * jax.nn.sigmoid(conv) * up).astype(BF16)
    out = jnp.dot(act_ref[...], wd_ref[...], preferred_element_type=F32)
    x = x + _rms(out, g_ref[2:3, :])
    o_ref[...] = x

    if has_next:
        hn = _rms(x, gn_ref[...]).astype(BF16)
        col = 0
        for p_ref in p_refs:
            width = p_ref.shape[1]
            for c in range(0, width, col_chunk):
                p_ref[:, c:c + col_chunk] = jnp.dot(
                    hn, win_ref[:, col + c:col + c + col_chunk],
                    preferred_element_type=F32).astype(p_ref.dtype)
            col += width


def layer_tail(ya, yb, x, w_out, gains, w_up, w_conv, b_conv, w_down, seq,
               next_gain=None, next_w_in=None, next_outs=(), tm=512, fc=256, col_chunk=512):
    n, d = x.shape
    d_ff = w_down.shape[0]
    has_next = next_w_in is not None
    assert d_ff % fc == 0 and seq % tm == 0
    const = lambda shape: pl.BlockSpec(shape, lambda i: (0, 0), pipeline_mode=pl.Buffered(1))
    row = lambda width: pl.BlockSpec((tm, width), lambda i: (i, 0))
    in_specs = [row(ya.shape[1]), row(yb.shape[1]), row(d), const(w_out.shape), const((3, d)),
                const(w_up.shape), const(w_conv.shape), const((1, d_ff)), const(w_down.shape)]
    args = [ya, yb, x, w_out, gains, w_up, w_conv, b_conv.reshape(1, d_ff), w_down]
    out_specs = [row(d)]
    out_shape = [jax.ShapeDtypeStruct((n, d), F32)]
    if has_next:
        in_specs += [const((1, d)), const(next_w_in.shape)]
        args += [next_gain.reshape(1, d), next_w_in]
        out_specs += [row(wd) for wd, _ in next_outs]
        out_shape += [jax.ShapeDtypeStruct((n, wd), dt) for wd, dt in next_outs]
    return pl.pallas_call(
        functools.partial(_layer_tail_kernel, tiles_per_seq=seq // tm, fc=fc,
                          col_chunk=col_chunk, has_next=has_next),
        grid=(n // tm,),
        in_specs=in_specs,
        out_specs=out_specs,
        out_shape=out_shape,
        scratch_shapes=[pltpu.VMEM((tm, d_ff), BF16),
                        pltpu.VMEM((2, tm + SUBLANES, fc), F32),
                        pltpu.VMEM((SUBLANES, d_ff), F32)],
        compiler_params=_cparams("arbitrary"),
        name="layer_tail",
    )(*args)


def kernel(x, norm_g, rel_bias, w_in_even, w_out_even, w_sc, w_in_odd, w_out_odd, w_cc, b_cc,
           ln_cc_g, ln_cc_b, w_up, w_ffn_conv, b_ffn_conv, w_down):
    batch, seq, d = x.shape
    depth = norm_g.shape[0]
    sc_width = w_sc.shape[2]
    cc_width = w_cc.shape[2]
    xf = x.reshape(batch * seq, d)
    bias = bias_tables(rel_bias)

    def in_proj_weights(layer):
        if layer % 2 == 0:
            w_in = w_in_even[layer // 2].astype(BF16)
            conv_cols = 3 * sc_width
            return w_in, [(conv_cols, BF16), (w_in.shape[1] - conv_cols, F32)]
        w_in = w_in_odd[layer // 2].astype(BF16)
        return w_in, [(w_in.shape[1], BF16)]

    w_in, outs = in_proj_weights(0)
    proj = norm_matmul(xf, norm_g[0, 0], w_in, outs)
    for layer in range(depth):
        i = layer // 2
        if layer % 2 == 0:
            pa, qkv = proj
            y_a = short_conv(pa, w_sc[i], batch, seq)
            y_b = dilated_attention(qkv, bias, batch, seq)
            w_out = w_out_even[i].astype(BF16)
        else:
            (p,) = proj
            y_a = conformer_conv(p, w_cc[i], b_cc[i], ln_cc_g[i], ln_cc_b[i], batch, seq)
            y_b = stick_breaking(p, 2 * cc_width, batch, seq)
            w_out = w_out_odd[i].astype(BF16)
        nxt = {}
        if layer + 1 < depth:
            w_in, outs = in_proj_weights(layer + 1)
            nxt = dict(next_gain=norm_g[layer + 1, 0], next_w_in=w_in, next_outs=outs)
        xf, *proj = layer_tail(y_a, y_b, xf, w_out, norm_g[layer, 1:4], w_up[layer].astype(BF16),
                               w_ffn_conv[layer], b_ffn_conv[layer], w_down[layer].astype(BF16),
                               seq, **nxt)
    return xf.reshape(batch, seq, d)
```

```python
import functools
import math

import numpy as np
import jax
import jax.numpy as jnp
from jax import lax
from jax.experimental import pallas as pl
from jax.experimental.pallas import tpu as pltpu

F32 = jnp.float32
BF16 = jnp.bfloat16

EPS = 1e-6
HEAD_DIM = 64
LANES = 128
SUBLANES = 8
HEADS_PER_TILE = LANES // HEAD_DIM
DA_PAIRS = ((128, 1), (512, 4), (2048, 16))
DA_SPAN = 128
MERGE_DIL = 4
BLOCK = 128
REL_BUCKETS = 32
REL_MAX_DIST = 2048
MASKED = -1e30
SB_EXIT = -105.0
VMEM_LIMIT = 56 * 1024 * 1024


def _cparams(*sem):
    return pltpu.CompilerParams(dimension_semantics=sem, vmem_limit_bytes=VMEM_LIMIT)


def _rms(x, g):
    return x * lax.rsqrt(jnp.mean(x * x, axis=-1, keepdims=True) + EPS) * g


def _norm_matmul_kernel(x_ref, g_ref, w_ref, *o_refs, col_chunk):
    tm = x_ref.shape[0]
    for rows in (slice(0, tm // 2), slice(tm // 2, tm)):
        h = _rms(x_ref[rows, :], g_ref[...]).astype(BF16)
        col = 0
        for o_ref in o_refs:
            width = o_ref.shape[1]
            for c in range(0, width, col_chunk):
                o_ref[rows, c:c + col_chunk] = jnp.dot(
                    h, w_ref[:, col + c:col + c + col_chunk],
                    preferred_element_type=F32).astype(o_ref.dtype)
            col += width


def norm_matmul(x, g, w, outs, tm=512, col_chunk=512):
    n, d = x.shape
    assert n % tm == 0 and sum(wd for wd, _ in outs) == w.shape[1]
    assert all(wd % col_chunk == 0 for wd, _ in outs)
    return pl.pallas_call(
        functools.partial(_norm_matmul_kernel, col_chunk=col_chunk),
        grid=(n // tm,),
        in_specs=[pl.BlockSpec((tm, d), lambda i: (i, 0)),
                  pl.BlockSpec((1, d), lambda i: (0, 0)),
                  pl.BlockSpec(w.shape, lambda i: (0, 0))],
        out_specs=[pl.BlockSpec((tm, wd), lambda i: (i, 0)) for wd, _ in outs],
        out_shape=[jax.ShapeDtypeStruct((n, wd), dt) for wd, dt in outs],
        compiler_params=_cparams("parallel"),
        name="norm_in_proj",
    )(x, g.reshape(1, d), w)


def _short_conv_kernel(gb_ref, gc_ref, xa_ref, w_ref, o_ref, pad_ref):
    ts = o_ref.shape[0]
    @pl.when(pl.program_id(1) == 0)
    def _():
        pad_ref[0:SUBLANES, :] = jnp.zeros((SUBLANES, pad_ref.shape[1]), F32)

    @pl.when(pl.program_id(1) != 0)
    def _():
        pad_ref[0:SUBLANES, :] = pad_ref[ts:ts + SUBLANES, :]

    c = gc_ref[...].astype(F32) * xa_ref[...].astype(F32)
    pad_ref[SUBLANES:, :] = c
    conv = (w_ref[2:3, :] * c
            + w_ref[1:2, :] * pad_ref[SUBLANES - 1:SUBLANES - 1 + ts, :]
            + w_ref[0:1, :] * pad_ref[SUBLANES - 2:SUBLANES - 2 + ts, :])
    o_ref[...] = (gb_ref[...].astype(F32) * conv).astype(o_ref.dtype)


def short_conv(pa, w_sc, batch, seq, ts=512):
    n = pa.shape[0]
    width = w_sc.shape[1]
    nt = seq // ts
    return pl.pallas_call(
        _short_conv_kernel,
        grid=(batch, nt),
        in_specs=[pl.BlockSpec((ts, width), lambda b, t: (b * nt + t, 0)),
                  pl.BlockSpec((ts, width), lambda b, t: (b * nt + t, 1)),
                  pl.BlockSpec((ts, width), lambda b, t: (b * nt + t, 2)),
                  pl.BlockSpec(w_sc.shape, lambda b, t: (0, 0))],
        out_specs=pl.BlockSpec((ts, width), lambda b, t: (b * nt + t, 0)),
        out_shape=jax.ShapeDtypeStruct((n, width), BF16),
        scratch_shapes=[pltpu.VMEM((ts + SUBLANES, width), F32)],
        compiler_params=_cparams("arbitrary", "arbitrary"),
        name="short_conv",
    )(pa, pa, pa, w_sc)


def _bucket_tables():
    rel = np.arange(BLOCK)[:, None] - np.arange(2 * BLOCK)[None, :] + DA_SPAN
    valid = (rel >= 0) & (rel <= DA_SPAN)
    max_exact = REL_BUCKETS // 2
    tabs = []
    for _, dil in DA_PAIRS:
        dist = np.clip(rel, 0, DA_SPAN) * dil
        d = np.maximum(dist, 1).astype(np.float32)
        large = max_exact + (np.log(d / np.float32(max_exact))
                             / np.float32(math.log(REL_MAX_DIST / max_exact))
                             * np.float32(REL_BUCKETS - max_exact)).astype(np.int32)
        large = np.minimum(large, REL_BUCKETS - 1)
        bucket = np.where(dist < max_exact, dist, large)
        tabs.append(np.where(valid, bucket, -1))
        tabs.append(np.where(valid & (np.arange(2 * BLOCK)[None, :] >= BLOCK), bucket, -1))
    return np.stack(tabs).astype(np.int32)


def _bias_table_kernel(rb_ref, idx_ref, o_ref):
    h = pl.program_id(1)
    idx = idx_ref[0]
    acc = jnp.full(idx.shape, MASKED, F32)
    for b in range(REL_BUCKETS):
        acc = jnp.where(idx == b, rb_ref[b, h], acc)
    o_ref[0, 0] = acc


def bias_tables(rel_bias):
    idx = jnp.asarray(_bucket_tables())
    heads = rel_bias.shape[1]
    return pl.pallas_call(
        _bias_table_kernel,
        grid=(idx.shape[0], heads),
        in_specs=[pl.BlockSpec(memory_space=pltpu.SMEM),
                  pl.BlockSpec((1, BLOCK, 2 * BLOCK), lambda g, h: (g, 0, 0))],
        out_specs=pl.BlockSpec((1, 1, BLOCK, 2 * BLOCK), lambda g, h: (g, h, 0, 0)),
        out_shape=jax.ShapeDtypeStruct((idx.shape[0], heads, BLOCK, 2 * BLOCK), F32),
        compiler_params=_cparams("arbitrary", "arbitrary"),
        name="bias_tables",
    )(rel_bias, idx)


def _regrouped_dilations():
    return [dil for _, dil in DA_PAIRS if dil > 1 and dil % SUBLANES == 0]


def _dilated_attn_kernel(q_ref, k_ref, v_ref, *rest, group):
    dilated = _regrouped_dilations()
    hbm_refs = rest[:len(dilated)]
    bias_ref, o_ref, m_ref, l_ref, u_ref, s0_ref, s1_ref, grp_ref, y_ref, sem = rest[len(dilated):]
    seq = q_ref.shape[0]
    width = hbm_refs[0].shape[-1] // 3
    b = pl.program_id(0)
    pair = pl.program_id(1)
    lane = lax.broadcasted_iota(jnp.int32, (1, LANES), 1)
    head_mask = (lane < HEAD_DIM, lane >= HEAD_DIM)
    nt_dims = (((1,), (1,)), ((), ()))
    ngroups = seq // BLOCK // group
    assert ngroups % 2 == 0

    def regroup_copy(n, a, r):
        rows = seq // dilated[n]
        cols = pl.ds(pl.multiple_of(a * width + pair * LANES, LANES), LANES)
        return pltpu.make_async_copy(hbm_refs[n].at[b, :, r, cols],
                                     grp_ref.at[n, a, pl.ds(r * rows, rows), :], sem.at[n, a])

    for n, dil in enumerate(dilated):
        for a in range(3):
            for r in range(dil):
                regroup_copy(n, a, r).start()

    for g, (_, dil) in enumerate(DA_PAIRS):
        nb = seq // dil // BLOCK
        srcs = (q_ref, k_ref, v_ref)
        if dil in dilated:
            n = dilated.index(dil)
            for a in range(3):
                for r in range(dil):
                    regroup_copy(n, a, r).wait()
            srcs = tuple(grp_ref.at[n, a] for a in range(3))

        def block_rows(blk):
            r = blk // nb
            j = blk % nb
            seq_rows = lambda jj: pl.ds(r + jj * (BLOCK * dil), BLOCK, stride=dil)
            own = pl.ds(pl.multiple_of(blk * BLOCK, BLOCK), BLOCK)
            prev = pl.ds(pl.multiple_of(jnp.maximum(blk - 1, 0) * BLOCK, BLOCK), BLOCK)
            if dil <= MERGE_DIL:
                out = own
            else:
                sub = dil // MERGE_DIL
                out = pl.ds((r % MERGE_DIL) * (seq // MERGE_DIL) + sub * BLOCK * j
                            + r // MERGE_DIL, BLOCK, stride=sub)
            if dil > 1 and dil not in dilated:
                own, prev = seq_rows(j), seq_rows(jnp.maximum(j - 1, 0))
            return own, prev, out, 2 * g + jnp.where(j == 0, 1, 0)

        def logits(grp, s_ref):
            for t in range(group):
                own, prev, _, table = block_rows(grp * group + t)
                q = srcs[0][own, :] * (1.0 / math.sqrt(HEAD_DIM))
                k2 = jnp.concatenate([srcs[1][prev, :], srcs[1][own, :]], axis=0).astype(BF16)
                for h in range(HEADS_PER_TILE):
                    qh = jnp.where(head_mask[h], q, 0.0).astype(BF16)
                    s = lax.dot_general(qh, k2, nt_dims, preferred_element_type=F32)
                    s_ref[HEADS_PER_TILE * t + h] = s + bias_ref[table, h]

        def finish(grp, s_ref):
            for t in range(group):
                own, prev, out, _ = block_rows(grp * group + t)
                v2 = jnp.concatenate([srcs[2][prev, :], srcs[2][own, :]], axis=0).astype(BF16)
                v2 = jnp.concatenate([v2, jnp.ones(v2.shape, BF16)], axis=1)
                stats = []
                for h in range(HEADS_PER_TILE):
                    s = s_ref[HEADS_PER_TILE * t + h]
                    m = jnp.max(s, axis=-1, keepdims=True)
                    p = jnp.exp(s - m)
                    pv = jnp.dot(p.astype(BF16), v2, preferred_element_type=F32)
                    stats.append((jnp.broadcast_to(m, (BLOCK, LANES)), pv[:, LANES:],
                                  pv[:, :LANES]))
                both = [jnp.where(head_mask[0], x0, x1) for x0, x1 in zip(*stats)]
                m_ref[g, out, :] = both[0]
                l_ref[g, out, :] = both[1]
                u_ref[g, out, :] = both[2]

        def body(i, carry):
            logits(2 * i + 1, s1_ref)
            finish(2 * i, s0_ref)
            logits(jnp.minimum(2 * i + 2, ngroups - 1), s0_ref)
            finish(2 * i + 1, s1_ref)
            return carry

        logits(jnp.int32(0), s0_ref)
        lax.fori_loop(0, ngroups // 2, body, 0)

    nbm = seq // MERGE_DIL // BLOCK

    def merge(c, carry):
        own = pl.ds(pl.multiple_of(c * BLOCK, BLOCK), BLOCK)
        in_seq = pl.ds(c // nbm + (c % nbm) * (BLOCK * MERGE_DIL), BLOCK, stride=MERGE_DIL)
        rows = [in_seq] + [own] * (len(DA_PAIRS) - 1)
        ms = [m_ref[g, rows[g], :] for g in range(len(DA_PAIRS))]
        top = functools.reduce(jnp.maximum, ms)
        wgts = [jnp.exp(m - top) for m in ms]
        num = sum(w * u_ref[g, rows[g], :] for g, w in enumerate(wgts))
        den = sum(w * l_ref[g, rows[g], :] for g, w in enumerate(wgts))
        y_ref[in_seq, :] = num / den
        return carry

    lax.fori_loop(0, seq // BLOCK, merge, 0)

    def emit(c, carry):
        rs = pl.ds(pl.multiple_of(c * BLOCK, BLOCK), BLOCK)
        o_ref[rs, :] = y_ref[rs, :].astype(o_ref.dtype)
        return carry

    lax.fori_loop(0, seq // BLOCK, emit, 0)


def dilated_attention(qkv, bias, batch, seq, group=4):
    n = qkv.shape[0]
    width = qkv.shape[1] // 3
    nhp = width // LANES
    dilated = _regrouped_dilations()
    blk = lambda off: pl.BlockSpec((seq, LANES), lambda b, p: (b, off + p))
    logits_buf = pltpu.VMEM((group * HEADS_PER_TILE, BLOCK, 2 * BLOCK), F32)
    by_residue = [qkv.reshape(batch, seq // dil, dil, qkv.shape[1]) for dil in dilated]
    return pl.pallas_call(
        functools.partial(_dilated_attn_kernel, group=group),
        grid=(batch, nhp),
        in_specs=[blk(0), blk(nhp), blk(2 * nhp)]
        + [pl.BlockSpec(memory_space=pl.ANY)] * len(dilated)
        + [pl.BlockSpec((bias.shape[0], HEADS_PER_TILE, BLOCK, 2 * BLOCK),
                        lambda b, p: (0, p, 0, 0))],
        out_specs=pl.BlockSpec((seq, LANES), lambda b, p: (b, p)),
        out_shape=jax.ShapeDtypeStruct((n, width), BF16),
        scratch_shapes=[pltpu.VMEM((len(DA_PAIRS), seq, LANES), F32)] * 3 + [logits_buf] * 2
        + [pltpu.VMEM((len(dilated), 3, seq, LANES), F32),
           pltpu.VMEM((seq, LANES), F32),
           pltpu.SemaphoreType.DMA((len(dilated), 3))],
        compiler_params=_cparams("parallel", "parallel"),
        name="dilated_attention",
    )(qkv, qkv, qkv, *by_residue, bias)


def _conformer_conv_kernel(a_ref, gate_ref, w_ref, b_ref, lg_ref, lb_ref, o_ref,
                           pad_ref, sh_ref, wb_ref, *, taps, halo, chunk):
    ts = o_ref.shape[0]

    @pl.when(pl.program_id(1) == 0)
    def _():
        pad_ref[0:halo, :] = jnp.zeros((halo, pad_ref.shape[1]), F32)

    @pl.when(pl.program_id(1) != 0)
    def _():
        pad_ref[0:halo, :] = pad_ref[ts:ts + halo, :]

    pad_ref[halo:, :] = a_ref[...].astype(F32) * jax.nn.sigmoid(gate_ref[...].astype(F32))

    lead = halo - (taps - 1)
    n_a = [len(range(b, taps, SUBLANES)) for b in range(SUBLANES)]
    width = o_ref.shape[1]
    for b in range(SUBLANES):
        groups = ts // SUBLANES + n_a[b] - 1
        win = pad_ref[lead + b:lead + b + groups * SUBLANES, :]
        sh_ref[b, 0:groups] = win.reshape(groups, SUBLANES, width)
    for j in range(taps):
        wb_ref[j] = jnp.broadcast_to(w_ref[j:j + 1, :], (SUBLANES, width))

    cg = chunk // SUBLANES
    for c0 in range(0, ts, chunk):
        acc = jnp.zeros((cg, SUBLANES, width), F32)
        for b in range(SUBLANES):
            for a in range(n_a[b]):
                g0 = c0 // SUBLANES + a
                acc = acc + wb_ref[SUBLANES * a + b][None] * sh_ref[b, g0:g0 + cg]
        y = acc.reshape(chunk, width) + b_ref[...]
        mu = jnp.mean(y, axis=-1, keepdims=True)
        yc = y - mu
        var = jnp.mean(yc * yc, axis=-1, keepdims=True)
        z = yc * lax.rsqrt(var + EPS) * lg_ref[...] + lb_ref[...]
        o_ref[c0:c0 + chunk, :] = (z * jax.nn.sigmoid(z)).astype(o_ref.dtype)


def conformer_conv(p, w_cc, b_cc, ln_g, ln_b, batch, seq, ts=256, chunk=32):
    n = p.shape[0]
    taps, width = w_cc.shape
    halo = -(-(taps - 1) // SUBLANES) * SUBLANES
    nt = seq // ts
    vec = lambda v: v.reshape(1, width)
    vspec = pl.BlockSpec((1, width), lambda b, t: (0, 0))
    return pl.pallas_call(
        functools.partial(_conformer_conv_kernel, taps=taps, halo=halo, chunk=chunk),
        grid=(batch, nt),
        in_specs=[pl.BlockSpec((ts, width), lambda b, t: (b * nt + t, 0)),
                  pl.BlockSpec((ts, width), lambda b, t: (b * nt + t, 1)),
                  pl.BlockSpec(w_cc.shape, lambda b, t: (0, 0)),
                  vspec, vspec, vspec],
        out_specs=pl.BlockSpec((ts, width), lambda b, t: (b * nt + t, 0)),
        out_shape=jax.ShapeDtypeStruct((n, width), BF16),
        scratch_shapes=[pltpu.VMEM((ts + halo, width), F32),
                        pltpu.VMEM((SUBLANES, (ts + halo) // SUBLANES, SUBLANES, width), F32),
                        pltpu.VMEM((taps, SUBLANES, width), F32)],
        compiler_params=_cparams("arbitrary", "arbitrary"),
        name="conformer_conv",
    )(p, p, w_cc, vec(b_cc), vec(ln_g), vec(ln_b))


def _suffix_sum_matrix():
    j = np.arange(BLOCK)[:, None]
    s = np.arange(BLOCK)[None, :]
    half = -np.concatenate([(j >= s).astype(np.float32), np.ones((BLOCK, BLOCK), np.float32)],
                           axis=1)
    return np.concatenate([half, half], axis=0)


def _stick_breaking_kernel(q_ref, k_ref, v_ref, u_ref, o_ref, qm_ref, run_ref, acc_ref, cat_ref,
                           *, fixed_steps):
    tq = q_ref.shape[0]
    nsub = tq // BLOCK
    qi = pl.program_id(2)
    lane = lax.broadcasted_iota(jnp.int32, (1, LANES), 1)
    head_mask = (lane < HEAD_DIM, lane >= HEAD_DIM)
    nt_dims = (((1,), (1,)), ((), ()))
    rows2 = HEADS_PER_TILE * BLOCK
    row = lax.broadcasted_iota(jnp.int32, (rows2, BLOCK), 0)
    col = lax.broadcasted_iota(jnp.int32, (rows2, BLOCK), 1)
    before = col < jnp.bitwise_and(row, BLOCK - 1)

    for r in range(nsub):
        q = q_ref[r * BLOCK:(r + 1) * BLOCK, :].astype(F32) * (1.0 / math.sqrt(HEAD_DIM))
        qm_ref[r] = jnp.concatenate(
            [jnp.where(head_mask[h], q, 0.0) for h in range(HEADS_PER_TILE)], axis=0).astype(BF16)
    run_ref[...] = jnp.zeros(run_ref.shape, F32)

    sign_bit = jnp.int32(-2 ** 31)

    def sweep(steps, first):
        plan = []
        for slot, d in enumerate(steps):
            kbs = [qi * nsub + r - d for r in range(nsub)]
            kss = [pl.ds(pl.multiple_of(jnp.maximum(kb, 0) * BLOCK, BLOCK), BLOCK) for kb in kbs]
            zs = [lax.dot_general(qm_ref[r], k_ref[kss[r], :], nt_dims,
                                  preferred_element_type=F32) for r in range(nsub)]
            plan.append((isinstance(d, int) and d == 0, kbs, kss, zs, cat_ref.at[slot]))
        all_sums = []
        for own_block, _, _, zs, cat in plan:
            for r, z in enumerate(zs):
                neg_abs = pltpu.bitcast(pltpu.bitcast(z, jnp.int32) | sign_bit, F32)
                neg_log_rest = jnp.maximum(z, 0.0) + jnp.log(1.0 + jnp.exp(neg_abs))
                if own_block:
                    neg_log_rest = jnp.where(before, neg_log_rest, 0.0)
                hi = neg_log_rest.astype(BF16)
                lo = (neg_log_rest - hi.astype(F32)).astype(BF16)
                cat[r * rows2:(r + 1) * rows2, :] = jnp.concatenate([hi, lo], axis=1)
            all_sums.append(jnp.dot(cat[...], u_ref[...], preferred_element_type=F32))
        weights = [[] for _ in range(nsub)]
        values = [[] for _ in range(nsub)]
        for (own_block, kbs, kss, zs, _), sums in zip(plan, all_sums):
            for r in range(nsub):
                sub = sums[r * rows2:(r + 1) * rows2]
                run = run_ref[r]
                a = jnp.exp(zs[r] + sub[:, 0:BLOCK] + run)
                if own_block:
                    a = jnp.where(before, a, 0.0)
                weights[r].append(a.astype(BF16))
                run_ref[r] = run + sub[:, BLOCK:2 * BLOCK]
                v_blk = v_ref[kss[r], :]
                values[r].append(v_blk if own_block else
                                 jnp.where(kbs[r] < 0, jnp.zeros_like(v_blk), v_blk))
        for r in range(nsub):
            out = jnp.dot(jnp.concatenate(weights[r], axis=1), jnp.concatenate(values[r], axis=0),
                          preferred_element_type=F32)
            acc_ref[r] = out if first else acc_ref[r] + out

    def still_open(d):
        return jnp.logical_and(d < (qi + 1) * nsub, jnp.max(run_ref[...]) > SB_EXIT)

    def body(c):
        d, _ = c
        sweep([d], False)
        return d + 1, still_open(d + 1)

    sweep(list(range(fixed_steps)), True)
    lax.while_loop(lambda c: c[1], body, (jnp.int32(fixed_steps), still_open(fixed_steps)))
    for r in range(nsub):
        o_ref[r * BLOCK:(r + 1) * BLOCK, :] = jnp.where(
            head_mask[0], acc_ref[r, 0:BLOCK, :], acc_ref[r, BLOCK:rows2, :]).astype(o_ref.dtype)


def stick_breaking(p, col0, batch, seq, tq=1024, fixed_steps=3):
    n = p.shape[0]
    width = (p.shape[1] - col0) // 3
    nhp = width // LANES
    c0 = col0 // LANES
    nq = seq // tq
    u = jnp.asarray(_suffix_sum_matrix(), BF16)
    return pl.pallas_call(
        functools.partial(_stick_breaking_kernel, fixed_steps=fixed_steps),
        grid=(batch, nhp, nq),
        in_specs=[pl.BlockSpec((tq, LANES), lambda b, p_, i: (b * nq + i, c0 + p_)),
                  pl.BlockSpec((seq, LANES), lambda b, p_, i: (b, c0 + nhp + p_)),
                  pl.BlockSpec((seq, LANES), lambda b, p_, i: (b, c0 + 2 * nhp + p_)),
                  pl.BlockSpec(u.shape, lambda b, p_, i: (0, 0))],
        out_specs=pl.BlockSpec((tq, LANES), lambda b, p_, i: (b * nq + i, p_)),
        out_shape=jax.ShapeDtypeStruct((n, width), BF16),
        scratch_shapes=[pltpu.VMEM((tq // BLOCK, HEADS_PER_TILE * BLOCK, LANES), BF16),
                        pltpu.VMEM((tq // BLOCK, HEADS_PER_TILE * BLOCK, LANES), F32),
                        pltpu.VMEM((tq // BLOCK, HEADS_PER_TILE * BLOCK, LANES), F32),
                        pltpu.VMEM((fixed_steps, HEADS_PER_TILE * tq, 2 * BLOCK), BF16)],
        compiler_params=_cparams("parallel", "parallel", "arbitrary"),
        name="stick_breaking",
    )(p, p, p, u)


def _layer_tail_kernel(ya_ref, yb_ref, x_ref, wo_ref, g_ref, wup_ref, wc_ref, bc_ref, wd_ref,
                       *rest, tiles_per_seq, fc, col_chunk, has_next, parts):
    if has_next:
        gn_ref, win_ref, o_ref, *p_refs, act_ref, pad_ref, carry_ref, h_ref = rest
    else:
        o_ref, act_ref, pad_ref, carry_ref, h_ref = rest
        p_refs = []
    tm = x_ref.shape[0]
    d_ff = wd_ref.shape[0]
    wa_rows = ya_ref.shape[1]

    @pl.when(pl.program_id(0) % tiles_per_seq == 0)
    def _():
        carry_ref[...] = jnp.zeros(carry_ref.shape, F32)

    tp = tm // parts
    row_parts = [slice(r0, r0 + tp) for r0 in range(0, tm, tp)]
    mixes = [jnp.dot(ya_ref[rows, :], wo_ref[0:wa_rows, :], preferred_element_type=F32)
             + jnp.dot(yb_ref[rows, :], wo_ref[wa_rows:, :], preferred_element_type=F32)
             for rows in row_parts]
    xs = [x_ref[rows, :] + _rms(mix, g_ref[0:1, :]) for rows, mix in zip(row_parts, mixes)]
    for rows, xp in zip(row_parts, xs):
        h_ref[rows, :] = _rms(xp, g_ref[1:2, :]).astype(BF16)

    for part, rows in enumerate(row_parts):
        h = h_ref[rows, :]
        for n, c0 in enumerate(range(0, d_ff, fc)):
            cols = slice(c0, c0 + fc)
            gate = jnp.dot(h, wup_ref[:, cols], preferred_element_type=F32)
            up = jnp.dot(h, wup_ref[:, d_ff + c0:d_ff + c0 + fc], preferred_element_type=F32)
            pad = pad_ref.at[(part * (d_ff // fc) + n) % pad_ref.shape[0]]
            pad[0:SUBLANES, :] = carry_ref[:, cols]
            pad[SUBLANES:, :] = gate
            carry_ref[:, cols] = gate[tp - SUBLANES:, :]
            conv = (wc_ref[2:3, cols] * gate
                    + wc_ref[1:2, cols] * pad[SUBLANES - 1:SUBLANES - 1 + tp, :]
                    + wc_ref[0:1, cols] * pad[SUBLANES - 2:SUBLANES - 2 + tp, :]
                    + bc_ref[:, cols])
            act_ref[rows, cols] = (conv * jax.nn.sigmoid(conv) * up).astype(BF16)
    outs = [jnp.dot(act_ref[rows, :], wd_ref[...], preferred_element_type=F32)
            for rows in row_parts]
    for rows, xp, out in zip(row_parts, xs, outs):
        xp = xp + _rms(out, g_ref[2:3, :])
        o_ref[rows, :] = xp
        if has_next:
            h_ref[rows, :] = _rms(xp, gn_ref[...]).astype(BF16)

    if has_next:
        for rows in row_parts:
            hn = h_ref[rows, :]
            col = 0
            for p_ref in p_refs:
                width = p_ref.shape[1]
                for c in range(0, width, col_chunk):
                    p_ref[rows, c:c + col_chunk] = jnp.dot(
                        hn, win_ref[:, col + c:col + c + col_chunk],
                        preferred_element_type=F32).astype(p_ref.dtype)
                col += width


def layer_tail(ya, yb, x, w_out, gains, w_up, w_conv, b_conv, w_down, seq,
               next_gain=None, next_w_in=None, next_outs=(), tm=512, fc=256, col_chunk=512,
               parts=2):
    n, d = x.shape
    d_ff = w_down.shape[0]
    has_next = next_w_in is not None
    assert d_ff % fc == 0 and seq % tm == 0
    const = lambda shape: pl.BlockSpec(shape, lambda i: (0, 0), pipeline_mode=pl.Buffered(1))
    row = lambda width: pl.BlockSpec((tm, width), lambda i: (i, 0))
    in_specs = [row(ya.shape[1]), row(yb.shape[1]), row(d), const(w_out.shape), const((3, d)),
                const(w_up.shape), const(w_conv.shape), const((1, d_ff)), const(w_down.shape)]
    args = [ya, yb, x, w_out, gains, w_up, w_conv, b_conv.reshape(1, d_ff), w_down]
    out_specs = [row(d)]
    out_shape = [jax.ShapeDtypeStruct((n, d), F32)]
    if has_next:
        in_specs += [const((1, d)), const(next_w_in.shape)]
        args += [next_gain.reshape(1, d), next_w_in]
        out_specs += [row(wd) for wd, _ in next_outs]
        out_shape += [jax.ShapeDtypeStruct((n, wd), dt) for wd, dt in next_outs]
    return pl.pallas_call(
        functools.partial(_layer_tail_kernel, tiles_per_seq=seq // tm, fc=fc,
                          col_chunk=col_chunk, has_next=has_next, parts=parts),
        grid=(n // tm,),
        in_specs=in_specs,
        out_specs=out_specs,
        out_shape=out_shape,
        scratch_shapes=[pltpu.VMEM((tm, d_ff), BF16),
                        pltpu.VMEM((2, tm // parts + SUBLANES, fc), F32),
                        pltpu.VMEM((SUBLANES, d_ff), F32),
                        pltpu.VMEM((tm, d), BF16)],
        compiler_params=_cparams("arbitrary"),
        name="layer_tail",
    )(*args)


def kernel(x, norm_g, rel_bias, w_in_even, w_out_even, w_sc, w_in_odd, w_out_odd, w_cc, b_cc,
           ln_cc_g, ln_cc_b, w_up, w_ffn_conv, b_ffn_conv, w_down):
    batch, seq, d = x.shape
    depth = norm_g.shape[0]
    sc_width = w_sc.shape[2]
    cc_width = w_cc.shape[2]
    xf = x.reshape(batch * seq, d)
    bias = bias_tables(rel_bias)

    def in_proj_weights(layer):
        if layer % 2 == 0:
            w_in = w_in_even[layer // 2].astype(BF16)
            conv_cols = 3 * sc_width
            return w_in, [(conv_cols, BF16), (w_in.shape[1] - conv_cols, F32)]
        w_in = w_in_odd[layer // 2].astype(BF16)
        return w_in, [(w_in.shape[1], BF16)]

    w_in, outs = in_proj_weights(0)
    proj = norm_matmul(xf, norm_g[0, 0], w_in, outs)
    for layer in range(depth):
        i = layer // 2
        if layer % 2 == 0:
            pa, qkv = proj
            y_a = short_conv(pa, w_sc[i], batch, seq)
            y_b = dilated_attention(qkv, bias, batch, seq)
            w_out = w_out_even[i].astype(BF16)
        else:
            (p,) = proj
            y_a = conformer_conv(p, w_cc[i], b_cc[i], ln_cc_g[i], ln_cc_b[i], batch, seq)
            y_b = stick_breaking(p, 2 * cc_width, batch, seq)
            w_out = w_out_odd[i].astype(BF16)
        nxt = {}
        if layer + 1 < depth:
            w_in, outs = in_proj_weights(layer + 1)
            nxt = dict(next_gain=norm_g[layer + 1, 0], next_w_in=w_in, next_outs=outs)
        xf, *proj = layer_tail(y_a, y_b, xf, w_out, norm_g[layer, 1:4], w_up[layer].astype(BF16),
                               w_ffn_conv[layer], b_ffn_conv[layer], w_down[layer].astype(BF16),
                               seq, **nxt)
    return xf.reshape(batch, seq, d)
```

```python
import functools
import math

import numpy as np
import jax
import jax.numpy as jnp
from jax import lax
from jax.experimental import pallas as pl
from jax.experimental.pallas import tpu as pltpu

F32 = jnp.float32
BF16 = jnp.bfloat16

EPS = 1e-6
HEAD_DIM = 64
LANES = 128
SUBLANES = 8
HEADS_PER_TILE = LANES // HEAD_DIM
DA_PAIRS = ((128, 1), (512, 4), (2048, 16))
DA_SPAN = 128
MERGE_DIL = 4
BLOCK = 128
REL_BUCKETS = 32
REL_MAX_DIST = 2048
MASKED = -1e30
SB_EXIT = -105.0
VMEM_LIMIT = 56 * 1024 * 1024


def _cparams(*sem):
    return pltpu.CompilerParams(dimension_semantics=sem, vmem_limit_bytes=VMEM_LIMIT)


def _rms(x, g):
    return x * lax.rsqrt(jnp.mean(x * x, axis=-1, keepdims=True) + EPS) * g


def _norm_matmul_kernel(x_ref, g_ref, w_ref, *o_refs, col_chunk):
    tm = x_ref.shape[0]
    for rows in (slice(0, tm // 2), slice(tm // 2, tm)):
        h = _rms(x_ref[rows, :], g_ref[...]).astype(BF16)
        col = 0
        for o_ref in o_refs:
            width = o_ref.shape[1]
            for c in range(0, width, col_chunk):
                o_ref[rows, c:c + col_chunk] = jnp.dot(
                    h, w_ref[:, col + c:col + c + col_chunk],
                    preferred_element_type=F32).astype(o_ref.dtype)
            col += width


def _layer_weight_spec(stack, index):
    return pl.BlockSpec((None,) + stack.shape[1:], lambda i: (index, 0, 0),
                        pipeline_mode=pl.Buffered(1))


def norm_matmul(x, g, w_stack, w_index, outs, tm=512, col_chunk=512):
    n, d = x.shape
    assert n % tm == 0 and sum(wd for wd, _ in outs) == w_stack.shape[2]
    assert all(wd % col_chunk == 0 for wd, _ in outs)
    return pl.pallas_call(
        functools.partial(_norm_matmul_kernel, col_chunk=col_chunk),
        grid=(n // tm,),
        in_specs=[pl.BlockSpec((tm, d), lambda i: (i, 0)),
                  pl.BlockSpec((1, d), lambda i: (0, 0)),
                  _layer_weight_spec(w_stack, w_index)],
        out_specs=[pl.BlockSpec((tm, wd), lambda i: (i, 0)) for wd, _ in outs],
        out_shape=[jax.ShapeDtypeStruct((n, wd), dt) for wd, dt in outs],
        compiler_params=_cparams("parallel"),
        name="norm_in_proj",
    )(x, g.reshape(1, d), w_stack)


def _short_conv_kernel(gb_ref, gc_ref, xa_ref, w_ref, o_ref, pad_ref):
    ts = o_ref.shape[0]
    @pl.when(pl.program_id(1) == 0)
    def _():
        pad_ref[0:SUBLANES, :] = jnp.zeros((SUBLANES, pad_ref.shape[1]), F32)

    @pl.when(pl.program_id(1) != 0)
    def _():
        pad_ref[0:SUBLANES, :] = pad_ref[ts:ts + SUBLANES, :]

    c = gc_ref[...].astype(F32) * xa_ref[...].astype(F32)
    pad_ref[SUBLANES:, :] = c
    conv = (w_ref[2:3, :] * c
            + w_ref[1:2, :] * pad_ref[SUBLANES - 1:SUBLANES - 1 + ts, :]
            + w_ref[0:1, :] * pad_ref[SUBLANES - 2:SUBLANES - 2 + ts, :])
    o_ref[...] = (gb_ref[...].astype(F32) * conv).astype(o_ref.dtype)


def short_conv(pa, w_sc, batch, seq, ts=2048):
    n = pa.shape[0]
    width = w_sc.shape[1]
    nt = seq // ts
    return pl.pallas_call(
        _short_conv_kernel,
        grid=(batch, nt),
        in_specs=[pl.BlockSpec((ts, width), lambda b, t: (b * nt + t, 0)),
                  pl.BlockSpec((ts, width), lambda b, t: (b * nt + t, 1)),
                  pl.BlockSpec((ts, width), lambda b, t: (b * nt + t, 2)),
                  pl.BlockSpec(w_sc.shape, lambda b, t: (0, 0))],
        out_specs=pl.BlockSpec((ts, width), lambda b, t: (b * nt + t, 0)),
        out_shape=jax.ShapeDtypeStruct((n, width), BF16),
        scratch_shapes=[pltpu.VMEM((ts + SUBLANES, width), F32)],
        compiler_params=_cparams("arbitrary", "arbitrary"),
        name="short_conv",
    )(pa, pa, pa, w_sc)


def _bucket_tables():
    rel = np.arange(BLOCK)[:, None] - np.arange(2 * BLOCK)[None, :] + DA_SPAN
    valid = (rel >= 0) & (rel <= DA_SPAN)
    max_exact = REL_BUCKETS // 2
    tabs = []
    for _, dil in DA_PAIRS:
        dist = np.clip(rel, 0, DA_SPAN) * dil
        d = np.maximum(dist, 1).astype(np.float32)
        large = max_exact + (np.log(d / np.float32(max_exact))
                             / np.float32(math.log(REL_MAX_DIST / max_exact))
                             * np.float32(REL_BUCKETS - max_exact)).astype(np.int32)
        large = np.minimum(large, REL_BUCKETS - 1)
        bucket = np.where(dist < max_exact, dist, large)
        tabs.append(np.where(valid, bucket, -1))
        tabs.append(np.where(valid & (np.arange(2 * BLOCK)[None, :] >= BLOCK), bucket, -1))
    return np.stack(tabs).astype(np.int32)


def _bias_table_kernel(rb_ref, idx_ref, o_ref):
    idx = idx_ref[0]
    for h in range(o_ref.shape[1]):
        acc = jnp.full(idx.shape, MASKED, F32)
        for b in range(REL_BUCKETS):
            acc = jnp.where(idx == b, rb_ref[b, h], acc)
        o_ref[0, h] = acc


def bias_tables(rel_bias):
    idx = jnp.asarray(_bucket_tables())
    heads = rel_bias.shape[1]
    return pl.pallas_call(
        _bias_table_kernel,
        grid=(idx.shape[0],),
        in_specs=[pl.BlockSpec(memory_space=pltpu.SMEM),
                  pl.BlockSpec((1, BLOCK, 2 * BLOCK), lambda g: (g, 0, 0))],
        out_specs=pl.BlockSpec((1, heads, BLOCK, 2 * BLOCK), lambda g: (g, 0, 0, 0)),
        out_shape=jax.ShapeDtypeStruct((idx.shape[0], heads, BLOCK, 2 * BLOCK), F32),
        compiler_params=_cparams("arbitrary"),
        name="bias_tables",
    )(rel_bias, idx)


def _regrouped_dilations():
    return [dil for _, dil in DA_PAIRS if dil > 1 and dil % SUBLANES == 0]


def _dilated_attn_kernel(q_ref, k_ref, v_ref, *rest, group):
    dilated = _regrouped_dilations()
    hbm_refs = rest[:len(dilated)]
    bias_ref, o_ref, m_ref, l_ref, u_ref, s0_ref, s1_ref, grp_ref, y_ref, sem = rest[len(dilated):]
    seq = q_ref.shape[0]
    width = hbm_refs[0].shape[-1] // 3
    b = pl.program_id(0)
    pair = pl.program_id(1)
    lane = lax.broadcasted_iota(jnp.int32, (1, LANES), 1)
    head_mask = (lane < HEAD_DIM, lane >= HEAD_DIM)
    nt_dims = (((1,), (1,)), ((), ()))
    ngroups = seq // BLOCK // group
    assert ngroups % 2 == 0

    def regroup_copy(n, a, r):
        rows = seq // dilated[n]
        cols = pl.ds(pl.multiple_of(a * width + pair * LANES, LANES), LANES)
        return pltpu.make_async_copy(hbm_refs[n].at[b, :, r, cols],
                                     grp_ref.at[n, a, pl.ds(r * rows, rows), :], sem.at[n, a])

    for n, dil in enumerate(dilated):
        for a in range(3):
            for r in range(dil):
                regroup_copy(n, a, r).start()

    for g, (_, dil) in enumerate(DA_PAIRS):
        nb = seq // dil // BLOCK
        srcs = (q_ref, k_ref, v_ref)
        if dil in dilated:
            n = dilated.index(dil)
            for a in range(3):
                for r in range(dil):
                    regroup_copy(n, a, r).wait()
            srcs = tuple(grp_ref.at[n, a] for a in range(3))

        def block_rows(blk):
            r = blk // nb
            j = blk % nb
            seq_rows = lambda jj: pl.ds(r + jj * (BLOCK * dil), BLOCK, stride=dil)
            own = pl.ds(pl.multiple_of(blk * BLOCK, BLOCK), BLOCK)
            prev = pl.ds(pl.multiple_of(jnp.maximum(blk - 1, 0) * BLOCK, BLOCK), BLOCK)
            if dil <= MERGE_DIL:
                out = own
            else:
                sub = dil // MERGE_DIL
                out = pl.ds((r % MERGE_DIL) * (seq // MERGE_DIL) + sub * BLOCK * j
                            + r // MERGE_DIL, BLOCK, stride=sub)
            if dil > 1 and dil not in dilated:
                own, prev = seq_rows(j), seq_rows(jnp.maximum(j - 1, 0))
            return own, prev, out, 2 * g + jnp.where(j == 0, 1, 0)

        def logits(grp, s_ref):
            for t in range(group):
                own, prev, _, table = block_rows(grp * group + t)
                q = srcs[0][own, :] * (1.0 / math.sqrt(HEAD_DIM))
                k2 = jnp.concatenate([srcs[1][prev, :], srcs[1][own, :]], axis=0).astype(BF16)
                for h in range(HEADS_PER_TILE):
                    qh = jnp.where(head_mask[h], q, 0.0).astype(BF16)
                    s = lax.dot_general(qh, k2, nt_dims, preferred_element_type=F32)
                    s_ref[HEADS_PER_TILE * t + h] = s + bias_ref[table, h]

        def finish(grp, s_ref):
            for t in range(group):
                own, prev, out, _ = block_rows(grp * group + t)
                v2 = jnp.concatenate([srcs[2][prev, :], srcs[2][own, :]], axis=0).astype(BF16)
                v2 = jnp.concatenate([v2, jnp.ones(v2.shape, BF16)], axis=1)
                stats = []
                for h in range(HEADS_PER_TILE):
                    s = s_ref[HEADS_PER_TILE * t + h]
                    m = jnp.max(s, axis=-1, keepdims=True)
                    p = jnp.exp(s - m)
                    pv = jnp.dot(p.astype(BF16), v2, preferred_element_type=F32)
                    stats.append((jnp.broadcast_to(m, (BLOCK, LANES)), pv[:, LANES:],
                                  pv[:, :LANES]))
                both = [jnp.where(head_mask[0], x0, x1) for x0, x1 in zip(*stats)]
                m_ref[g, out, :] = both[0]
                l_ref[g, out, :] = both[1]
                u_ref[g, out, :] = both[2]

        def body(i, carry):
            logits(2 * i + 1, s1_ref)
            finish(2 * i, s0_ref)
            logits(jnp.minimum(2 * i + 2, ngroups - 1), s0_ref)
            finish(2 * i + 1, s1_ref)
            return carry

        logits(jnp.int32(0), s0_ref)
        lax.fori_loop(0, ngroups // 2, body, 0)

    nbm = seq // MERGE_DIL // BLOCK

    def merge(c, carry):
        own = pl.ds(pl.multiple_of(c * BLOCK, BLOCK), BLOCK)
        in_seq = pl.ds(c // nbm + (c % nbm) * (BLOCK * MERGE_DIL), BLOCK, stride=MERGE_DIL)
        rows = [in_seq] + [own] * (len(DA_PAIRS) - 1)
        ms = [m_ref[g, rows[g], :] for g in range(len(DA_PAIRS))]
        top = functools.reduce(jnp.maximum, ms)
        wgts = [jnp.exp(m - top) for m in ms]
        num = sum(w * u_ref[g, rows[g], :] for g, w in enumerate(wgts))
        den = sum(w * l_ref[g, rows[g], :] for g, w in enumerate(wgts))
        y_ref[in_seq, :] = num / den
        return carry

    lax.fori_loop(0, seq // BLOCK, merge, 0)

    def emit(c, carry):
        rs = pl.ds(pl.multiple_of(c * BLOCK, BLOCK), BLOCK)
        o_ref[rs, :] = y_ref[rs, :].astype(o_ref.dtype)
        return carry

    lax.fori_loop(0, seq // BLOCK, emit, 0)


def dilated_attention(qkv, bias, batch, seq, group=4):
    n = qkv.shape[0]
    width = qkv.shape[1] // 3
    nhp = width // LANES
    dilated = _regrouped_dilations()
    blk = lambda off: pl.BlockSpec((seq, LANES), lambda b, p: (b, off + p))
    logits_buf = pltpu.VMEM((group * HEADS_PER_TILE, BLOCK, 2 * BLOCK), F32)
    by_residue = [qkv.reshape(batch, seq // dil, dil, qkv.shape[1]) for dil in dilated]
    return pl.pallas_call(
        functools.partial(_dilated_attn_kernel, group=group),
        grid=(batch, nhp),
        in_specs=[blk(0), blk(nhp), blk(2 * nhp)]
        + [pl.BlockSpec(memory_space=pl.ANY)] * len(dilated)
        + [pl.BlockSpec((bias.shape[0], HEADS_PER_TILE, BLOCK, 2 * BLOCK),
                        lambda b, p: (0, p, 0, 0))],
        out_specs=pl.BlockSpec((seq, LANES), lambda b, p: (b, p)),
        out_shape=jax.ShapeDtypeStruct((n, width), BF16),
        scratch_shapes=[pltpu.VMEM((len(DA_PAIRS), seq, LANES), F32)] * 3 + [logits_buf] * 2
        + [pltpu.VMEM((len(dilated), 3, seq, LANES), F32),
           pltpu.VMEM((seq, LANES), F32),
           pltpu.SemaphoreType.DMA((len(dilated), 3))],
        compiler_params=_cparams("parallel", "parallel"),
        name="dilated_attention",
    )(qkv, qkv, qkv, *by_residue, bias)


def _conformer_conv_kernel(a_ref, gate_ref, w_ref, b_ref, lg_ref, lb_ref, o_ref,
                           pad_ref, sh_ref, wb_ref, *, taps, halo, chunk):
    ts = o_ref.shape[0]

    @pl.when(pl.program_id(1) == 0)
    def _():
        pad_ref[0:halo, :] = jnp.zeros((halo, pad_ref.shape[1]), F32)

    @pl.when(pl.program_id(1) != 0)
    def _():
        pad_ref[0:halo, :] = pad_ref[ts:ts + halo, :]

    pad_ref[halo:, :] = a_ref[...].astype(F32) * jax.nn.sigmoid(gate_ref[...].astype(F32))

    lead = halo - (taps - 1)
    n_a = [len(range(b, taps, SUBLANES)) for b in range(SUBLANES)]
    width = o_ref.shape[1]
    for b in range(SUBLANES):
        groups = ts // SUBLANES + n_a[b] - 1
        win = pad_ref[lead + b:lead + b + groups * SUBLANES, :]
        sh_ref[b, 0:groups] = win.reshape(groups, SUBLANES, width)
    for j in range(taps):
        wb_ref[j] = jnp.broadcast_to(w_ref[j:j + 1, :], (SUBLANES, width))

    cg = chunk // SUBLANES
    for c0 in range(0, ts, chunk):
        acc = jnp.zeros((cg, SUBLANES, width), F32)
        for b in range(SUBLANES):
            for a in range(n_a[b]):
                g0 = c0 // SUBLANES + a
                acc = acc + wb_ref[SUBLANES * a + b][None] * sh_ref[b, g0:g0 + cg]
        y = acc.reshape(chunk, width) + b_ref[...]
        mu = jnp.mean(y, axis=-1, keepdims=True)
        yc = y - mu
        var = jnp.mean(yc * yc, axis=-1, keepdims=True)
        z = yc * lax.rsqrt(var + EPS) * lg_ref[...] + lb_ref[...]
        o_ref[c0:c0 + chunk, :] = (z * jax.nn.sigmoid(z)).astype(o_ref.dtype)


def conformer_conv(p, w_cc, b_cc, ln_g, ln_b, batch, seq, ts=256, chunk=32):
    n = p.shape[0]
    taps, width = w_cc.shape
    halo = -(-(taps - 1) // SUBLANES) * SUBLANES
    nt = seq // ts
    vec = lambda v: v.reshape(1, width)
    vspec = pl.BlockSpec((1, width), lambda b, t: (0, 0))
    return pl.pallas_call(
        functools.partial(_conformer_conv_kernel, taps=taps, halo=halo, chunk=chunk),
        grid=(batch, nt),
        in_specs=[pl.BlockSpec((ts, width), lambda b, t: (b * nt + t, 0)),
                  pl.BlockSpec((ts, width), lambda b, t: (b * nt + t, 1)),
                  pl.BlockSpec(w_cc.shape, lambda b, t: (0, 0)),
                  vspec, vspec, vspec],
        out_specs=pl.BlockSpec((ts, width), lambda b, t: (b * nt + t, 0)),
        out_shape=jax.ShapeDtypeStruct((n, width), BF16),
        scratch_shapes=[pltpu.VMEM((ts + halo, width), F32),
                        pltpu.VMEM((SUBLANES, (ts + halo) // SUBLANES, SUBLANES, width), F32),
                        pltpu.VMEM((taps, SUBLANES, width), F32)],
        compiler_params=_cparams("arbitrary", "arbitrary"),
        name="conformer_conv",
    )(p, p, w_cc, vec(b_cc), vec(ln_g), vec(ln_b))


def _suffix_sum_matrix():
    j = np.arange(BLOCK)[:, None]
    s = np.arange(BLOCK)[None, :]
    half = -np.concatenate([(j >= s).astype(np.float32), np.ones((BLOCK, BLOCK), np.float32)],
                           axis=1)
    return np.concatenate([half, half], axis=0)


def _stick_breaking_kernel(q_ref, k_ref, v_ref, u_ref, o_ref, qm_ref, run_ref, acc_ref, cat_ref,
                           *, fixed_steps):
    tq = q_ref.shape[0]
    nsub = tq // BLOCK
    qi = pl.program_id(2)
    lane = lax.broadcasted_iota(jnp.int32, (1, LANES), 1)
    head_mask = (lane < HEAD_DIM, lane >= HEAD_DIM)
    nt_dims = (((1,), (1,)), ((), ()))
    rows2 = HEADS_PER_TILE * BLOCK
    row = lax.broadcasted_iota(jnp.int32, (rows2, BLOCK), 0)
    col = lax.broadcasted_iota(jnp.int32, (rows2, BLOCK), 1)
    before = col < jnp.bitwise_and(row, BLOCK - 1)

    for r in range(nsub):
        q = q_ref[r * BLOCK:(r + 1) * BLOCK, :].astype(F32) * (1.0 / math.sqrt(HEAD_DIM))
        qm_ref[r] = jnp.concatenate(
            [jnp.where(head_mask[h], q, 0.0) for h in range(HEADS_PER_TILE)], axis=0).astype(BF16)
    run_ref[...] = jnp.zeros(run_ref.shape, F32)

    sign_bit = jnp.int32(-2 ** 31)

    def sweep(steps, first):
        plan = []
        for slot, d in enumerate(steps):
            kbs = [qi * nsub + r - d for r in range(nsub)]
            kss = [pl.ds(pl.multiple_of(jnp.maximum(kb, 0) * BLOCK, BLOCK), BLOCK) for kb in kbs]
            zs = [lax.dot_general(qm_ref[r], k_ref[kss[r], :], nt_dims,
                                  preferred_element_type=F32) for r in range(nsub)]
            plan.append((isinstance(d, int) and d == 0, kbs, kss, zs, cat_ref.at[slot]))
        all_sums = []
        for own_block, _, _, zs, cat in plan:
            for r, z in enumerate(zs):
                neg_abs = pltpu.bitcast(pltpu.bitcast(z, jnp.int32) | sign_bit, F32)
                neg_log_rest = jnp.maximum(z, 0.0) + jnp.log(1.0 + jnp.exp(neg_abs))
                if own_block:
                    neg_log_rest = jnp.where(before, neg_log_rest, 0.0)
                hi = neg_log_rest.astype(BF16)
                lo = (neg_log_rest - hi.astype(F32)).astype(BF16)
                cat[r * rows2:(r + 1) * rows2, :] = jnp.concatenate([hi, lo], axis=1)
            all_sums.append(jnp.dot(cat[...], u_ref[...], preferred_element_type=F32))
        weights = [[] for _ in range(nsub)]
        values = [[] for _ in range(nsub)]
        for (own_block, kbs, kss, zs, _), sums in zip(plan, all_sums):
            for r in range(nsub):
                sub = sums[r * rows2:(r + 1) * rows2]
                run = run_ref[r]
                a = jnp.exp(zs[r] + sub[:, 0:BLOCK] + run)
                if own_block:
                    a = jnp.where(before, a, 0.0)
                weights[r].append(a.astype(BF16))
                run_ref[r] = run + sub[:, BLOCK:2 * BLOCK]
                v_blk = v_ref[kss[r], :]
                values[r].append(v_blk if own_block else
                                 jnp.where(kbs[r] < 0, jnp.zeros_like(v_blk), v_blk))
        for r in range(nsub):
            out = jnp.dot(jnp.concatenate(weights[r], axis=1), jnp.concatenate(values[r], axis=0),
                          preferred_element_type=F32)
            acc_ref[r] = out if first else acc_ref[r] + out

    def still_open(d):
        return jnp.logical_and(d < (qi + 1) * nsub, jnp.max(run_ref[...]) > SB_EXIT)

    def body(c):
        d, _ = c
        sweep([d], False)
        return d + 1, still_open(d + 1)

    sweep(list(range(fixed_steps)), True)
    lax.while_loop(lambda c: c[1], body, (jnp.int32(fixed_steps), still_open(fixed_steps)))
    for r in range(nsub):
        o_ref[r * BLOCK:(r + 1) * BLOCK, :] = jnp.where(
            head_mask[0], acc_ref[r, 0:BLOCK, :], acc_ref[r, BLOCK:rows2, :]).astype(o_ref.dtype)


def stick_breaking(p, col0, batch, seq, tq=1024, fixed_steps=3):
    n = p.shape[0]
    width = (p.shape[1] - col0) // 3
    nhp = width // LANES
    c0 = col0 // LANES
    nq = seq // tq
    u = jnp.asarray(_suffix_sum_matrix(), BF16)
    return pl.pallas_call(
        functools.partial(_stick_breaking_kernel, fixed_steps=fixed_steps),
        grid=(batch, nhp, nq),
        in_specs=[pl.BlockSpec((tq, LANES), lambda b, p_, i: (b * nq + i, c0 + p_)),
                  pl.BlockSpec((seq, LANES), lambda b, p_, i: (b, c0 + nhp + p_)),
                  pl.BlockSpec((seq, LANES), lambda b, p_, i: (b, c0 + 2 * nhp + p_)),
                  pl.BlockSpec(u.shape, lambda b, p_, i: (0, 0))],
        out_specs=pl.BlockSpec((tq, LANES), lambda b, p_, i: (b * nq + i, p_)),
        out_shape=jax.ShapeDtypeStruct((n, width), BF16),
        scratch_shapes=[pltpu.VMEM((tq // BLOCK, HEADS_PER_TILE * BLOCK, LANES), BF16),
                        pltpu.VMEM((tq // BLOCK, HEADS_PER_TILE * BLOCK, LANES), F32),
                        pltpu.VMEM((tq // BLOCK, HEADS_PER_TILE * BLOCK, LANES), F32),
                        pltpu.VMEM((fixed_steps, HEADS_PER_TILE * tq, 2 * BLOCK), BF16)],
        compiler_params=_cparams("parallel", "parallel", "arbitrary"),
        name="stick_breaking",
    )(p, p, p, u)


def _layer_tail_kernel(ya_ref, yb_ref, x_ref, wo_ref, g_ref, wup_ref, wc_ref, bc_ref, wd_ref,
                       *rest, tiles_per_seq, fc, col_chunk, has_next, parts):
    if has_next:
        gn_ref, win_ref, o_ref, *p_refs, act_ref, pad_ref, carry_ref, h_ref = rest
    else:
        o_ref, act_ref, pad_ref, carry_ref, h_ref = rest
        p_refs = []
    tm = x_ref.shape[0]
    d_ff = wd_ref.shape[0]
    wa_rows = ya_ref.shape[1]

    @pl.when(pl.program_id(0) % tiles_per_seq == 0)
    def _():
        carry_ref[...] = jnp.zeros(carry_ref.shape, F32)

    tp = tm // parts
    row_parts = [slice(r0, r0 + tp) for r0 in range(0, tm, tp)]
    mixes = [jnp.dot(ya_ref[rows, :], wo_ref[0:wa_rows, :], preferred_element_type=F32)
             + jnp.dot(yb_ref[rows, :], wo_ref[wa_rows:, :], preferred_element_type=F32)
             for rows in row_parts]
    xs = [x_ref[rows, :] + _rms(mix, g_ref[0:1, :]) for rows, mix in zip(row_parts, mixes)]
    for rows, xp in zip(row_parts, xs):
        h_ref[rows, :] = _rms(xp, g_ref[1:2, :]).astype(BF16)

    for part, rows in enumerate(row_parts):
        h = h_ref[rows, :]
        for n, c0 in enumerate(range(0, d_ff, fc)):
            cols = slice(c0, c0 + fc)
            gate = jnp.dot(h, wup_ref[:, cols], preferred_element_type=F32)
            up = jnp.dot(h, wup_ref[:, d_ff + c0:d_ff + c0 + fc], preferred_element_type=F32)
            pad = pad_ref.at[(part * (d_ff // fc) + n) % pad_ref.shape[0]]
            pad[0:SUBLANES, :] = carry_ref[:, cols]
            pad[SUBLANES:, :] = gate
            carry_ref[:, cols] = gate[tp - SUBLANES:, :]
            conv = (wc_ref[2:3, cols] * gate
                    + wc_ref[1:2, cols] * pad[SUBLANES - 1:SUBLANES - 1 + tp, :]
                    + wc_ref[0:1, cols] * pad[SUBLANES - 2:SUBLANES - 2 + tp, :]
                    + bc_ref[:, cols])
            act_ref[rows, cols] = (conv * jax.nn.sigmoid(conv) * up).astype(BF16)
    outs = [jnp.dot(act_ref[rows, :], wd_ref[...], preferred_element_type=F32)
            for rows in row_parts]
    for rows, xp, out in zip(row_parts, xs, outs):
        xp = xp + _rms(out, g_ref[2:3, :])
        o_ref[rows, :] = xp
        if has_next:
            h_ref[rows, :] = _rms(xp, gn_ref[...]).astype(BF16)

    if has_next:
        for rows in row_parts:
            hn = h_ref[rows, :]
            col = 0
            for p_ref in p_refs:
                width = p_ref.shape[1]
                for c in range(0, width, col_chunk):
                    p_ref[rows, c:c + col_chunk] = jnp.dot(
                        hn, win_ref[:, col + c:col + c + col_chunk],
                        preferred_element_type=F32).astype(p_ref.dtype)
                col += width


def layer_tail(ya, yb, x, w_out, gains, w_up, w_conv, b_conv, w_down, seq,
               next_gain=None, next_w_in=None, next_outs=(), tm=512, fc=256, col_chunk=512,
               parts=2):
    n, d = x.shape
    d_ff = w_down[0].shape[1]
    has_next = next_w_in is not None
    assert d_ff % fc == 0 and seq % tm == 0
    const = lambda shape: pl.BlockSpec(shape, lambda i: (0, 0), pipeline_mode=pl.Buffered(1))
    row = lambda width: pl.BlockSpec((tm, width), lambda i: (i, 0))
    in_specs = [row(ya.shape[1]), row(yb.shape[1]), row(d), _layer_weight_spec(*w_out),
                const((3, d)), _layer_weight_spec(*w_up), const(w_conv.shape), const((1, d_ff)),
                _layer_weight_spec(*w_down)]
    args = [ya, yb, x, w_out[0], gains, w_up[0], w_conv, b_conv.reshape(1, d_ff), w_down[0]]
    out_specs = [row(d)]
    out_shape = [jax.ShapeDtypeStruct((n, d), F32)]
    if has_next:
        in_specs += [const((1, d)), _layer_weight_spec(*next_w_in)]
        args += [next_gain.reshape(1, d), next_w_in[0]]
        out_specs += [row(wd) for wd, _ in next_outs]
        out_shape += [jax.ShapeDtypeStruct((n, wd), dt) for wd, dt in next_outs]
    return pl.pallas_call(
        functools.partial(_layer_tail_kernel, tiles_per_seq=seq // tm, fc=fc,
                          col_chunk=col_chunk, has_next=has_next, parts=parts),
        grid=(n // tm,),
        in_specs=in_specs,
        out_specs=out_specs,
        out_shape=out_shape,
        scratch_shapes=[pltpu.VMEM((tm, d_ff), BF16),
                        pltpu.VMEM((2, tm // parts + SUBLANES, fc), F32),
                        pltpu.VMEM((SUBLANES, d_ff), F32),
                        pltpu.VMEM((tm, d), BF16)],
        compiler_params=_cparams("arbitrary"),
        name="layer_tail",
    )(*args)


def kernel(x, norm_g, rel_bias, w_in_even, w_out_even, w_sc, w_in_odd, w_out_odd, w_cc, b_cc,
           ln_cc_g, ln_cc_b, w_up, w_ffn_conv, b_ffn_conv, w_down):
    batch, seq, d = x.shape
    depth = norm_g.shape[0]
    sc_width = w_sc.shape[2]
    cc_width = w_cc.shape[2]
    xf = x.reshape(batch * seq, d)
    bias = bias_tables(rel_bias)

    stacks = {"in_even": w_in_even, "in_odd": w_in_odd, "out_even": w_out_even,
              "out_odd": w_out_odd, "up": w_up, "down": w_down}
    stacks = {name: w.astype(BF16) for name, w in stacks.items()}

    def in_proj(layer):
        if layer % 2 == 0:
            conv_cols = 3 * sc_width
            w_in = (stacks["in_even"], layer // 2)
            return w_in, [(conv_cols, BF16), (w_in[0].shape[2] - conv_cols, F32)]
        w_in = (stacks["in_odd"], layer // 2)
        return w_in, [(w_in[0].shape[2], BF16)]

    w_in, outs = in_proj(0)
    proj = norm_matmul(xf, norm_g[0, 0], *w_in, outs)
    for layer in range(depth):
        i = layer // 2
        if layer % 2 == 0:
            pa, qkv = proj
            y_a = short_conv(pa, w_sc[i], batch, seq)
            y_b = dilated_attention(qkv, bias, batch, seq)
            w_out = (stacks["out_even"], i)
        else:
            (p,) = proj
            y_a = conformer_conv(p, w_cc[i], b_cc[i], ln_cc_g[i], ln_cc_b[i], batch, seq)
            y_b = stick_breaking(p, 2 * cc_width, batch, seq)
            w_out = (stacks["out_odd"], i)
        nxt = {}
        if layer + 1 < depth:
            w_in, outs = in_proj(layer + 1)
            nxt = dict(next_gain=norm_g[layer + 1, 0], next_w_in=w_in, next_outs=outs)
        xf, *proj = layer_tail(y_a, y_b, xf, w_out, norm_g[layer, 1:4], (stacks["up"], layer),
                               w_ffn_conv[layer], b_ffn_conv[layer], (stacks["down"], layer),
                               seq, **nxt)
    return xf.reshape(batch, seq, d)
```

```python
import functools
import math

import numpy as np
import jax
import jax.numpy as jnp
from jax import lax
from jax.experimental import pallas as pl
from jax.experimental.pallas import tpu as pltpu

F32 = jnp.float32
BF16 = jnp.bfloat16

EPS = 1e-6
HEAD_DIM = 64
LANES = 128
SUBLANES = 8
HEADS_PER_TILE = LANES // HEAD_DIM
DA_PAIRS = ((128, 1), (512, 4), (2048, 16))
DA_SPAN = 128
MERGE_DIL = 4
BLOCK = 128
REL_BUCKETS = 32
REL_MAX_DIST = 2048
MASKED = -1e30
SB_EXIT = -105.0
VMEM_LIMIT = 56 * 1024 * 1024


def _cparams(*sem):
    return pltpu.CompilerParams(dimension_semantics=sem, vmem_limit_bytes=VMEM_LIMIT)


def _rms(x, g):
    return x * lax.rsqrt(jnp.mean(x * x, axis=-1, keepdims=True) + EPS) * g


def _norm_matmul_kernel(x_ref, g_ref, w_ref, *o_refs, col_chunk):
    tm = x_ref.shape[0]
    for rows in (slice(0, tm // 2), slice(tm // 2, tm)):
        h = _rms(x_ref[rows, :], g_ref[...]).astype(BF16)
        col = 0
        for o_ref in o_refs:
            width = o_ref.shape[1]
            for c in range(0, width, col_chunk):
                o_ref[rows, c:c + col_chunk] = jnp.dot(
                    h, w_ref[:, col + c:col + c + col_chunk],
                    preferred_element_type=F32).astype(o_ref.dtype)
            col += width


def _layer_weight_spec(stack, index):
    return pl.BlockSpec((None,) + stack.shape[1:], lambda i: (index, 0, 0),
                        pipeline_mode=pl.Buffered(1))


def norm_matmul(x, g, w_stack, w_index, outs, tm=512, col_chunk=512):
    n, d = x.shape
    assert n % tm == 0 and sum(wd for wd, _ in outs) == w_stack.shape[2]
    assert all(wd % col_chunk == 0 for wd, _ in outs)
    return pl.pallas_call(
        functools.partial(_norm_matmul_kernel, col_chunk=col_chunk),
        grid=(n // tm,),
        in_specs=[pl.BlockSpec((tm, d), lambda i: (i, 0)),
                  pl.BlockSpec((1, d), lambda i: (0, 0)),
                  _layer_weight_spec(w_stack, w_index)],
        out_specs=[pl.BlockSpec((tm, wd), lambda i: (i, 0)) for wd, _ in outs],
        out_shape=[jax.ShapeDtypeStruct((n, wd), dt) for wd, dt in outs],
        compiler_params=_cparams("parallel"),
        name="norm_in_proj",
    )(x, g.reshape(1, d), w_stack)


def _short_conv_kernel(gb_ref, gc_ref, xa_ref, w_ref, o_ref, pad_ref):
    ts = o_ref.shape[0]
    @pl.when(pl.program_id(1) == 0)
    def _():
        pad_ref[0:SUBLANES, :] = jnp.zeros((SUBLANES, pad_ref.shape[1]), F32)

    @pl.when(pl.program_id(1) != 0)
    def _():
        pad_ref[0:SUBLANES, :] = pad_ref[ts:ts + SUBLANES, :]

    c = gc_ref[...].astype(F32) * xa_ref[...].astype(F32)
    pad_ref[SUBLANES:, :] = c
    conv = (w_ref[2:3, :] * c
            + w_ref[1:2, :] * pad_ref[SUBLANES - 1:SUBLANES - 1 + ts, :]
            + w_ref[0:1, :] * pad_ref[SUBLANES - 2:SUBLANES - 2 + ts, :])
    o_ref[...] = (gb_ref[...].astype(F32) * conv).astype(o_ref.dtype)


def short_conv(pa, w_sc, batch, seq, ts=2048):
    n = pa.shape[0]
    width = w_sc.shape[1]
    nt = seq // ts
    return pl.pallas_call(
        _short_conv_kernel,
        grid=(batch, nt),
        in_specs=[pl.BlockSpec((ts, width), lambda b, t: (b * nt + t, 0)),
                  pl.BlockSpec((ts, width), lambda b, t: (b * nt + t, 1)),
                  pl.BlockSpec((ts, width), lambda b, t: (b * nt + t, 2)),
                  pl.BlockSpec(w_sc.shape, lambda b, t: (0, 0))],
        out_specs=pl.BlockSpec((ts, width), lambda b, t: (b * nt + t, 0)),
        out_shape=jax.ShapeDtypeStruct((n, width), BF16),
        scratch_shapes=[pltpu.VMEM((ts + SUBLANES, width), F32)],
        compiler_params=_cparams("arbitrary", "arbitrary"),
        name="short_conv",
    )(pa, pa, pa, w_sc)


def _bucket_tables():
    rel = np.arange(BLOCK)[:, None] - np.arange(2 * BLOCK)[None, :] + DA_SPAN
    valid = (rel >= 0) & (rel <= DA_SPAN)
    max_exact = REL_BUCKETS // 2
    tabs = []
    for _, dil in DA_PAIRS:
        dist = np.clip(rel, 0, DA_SPAN) * dil
        d = np.maximum(dist, 1).astype(np.float32)
        large = max_exact + (np.log(d / np.float32(max_exact))
                             / np.float32(math.log(REL_MAX_DIST / max_exact))
                             * np.float32(REL_BUCKETS - max_exact)).astype(np.int32)
        large = np.minimum(large, REL_BUCKETS - 1)
        bucket = np.where(dist < max_exact, dist, large)
        tabs.append(np.where(valid, bucket, -1))
        tabs.append(np.where(valid & (np.arange(2 * BLOCK)[None, :] >= BLOCK), bucket, -1))
    return np.stack(tabs).astype(np.int32)


def _bias_table_kernel(rb_ref, idx_ref, o_ref):
    idx = idx_ref[0]
    for h in range(o_ref.shape[1]):
        acc = jnp.full(idx.shape, MASKED, F32)
        for b in range(REL_BUCKETS):
            acc = jnp.where(idx == b, rb_ref[b, h], acc)
        o_ref[0, h] = acc


def bias_tables(rel_bias):
    idx = jnp.asarray(_bucket_tables())
    heads = rel_bias.shape[1]
    return pl.pallas_call(
        _bias_table_kernel,
        grid=(idx.shape[0],),
        in_specs=[pl.BlockSpec(memory_space=pltpu.SMEM),
                  pl.BlockSpec((1, BLOCK, 2 * BLOCK), lambda g: (g, 0, 0))],
        out_specs=pl.BlockSpec((1, heads, BLOCK, 2 * BLOCK), lambda g: (g, 0, 0, 0)),
        out_shape=jax.ShapeDtypeStruct((idx.shape[0], heads, BLOCK, 2 * BLOCK), F32),
        compiler_params=_cparams("arbitrary"),
        name="bias_tables",
    )(rel_bias, idx)


def _regrouped_dilations():
    return [dil for _, dil in DA_PAIRS if dil > 1 and dil % SUBLANES == 0]


def _dilated_attn_kernel(q_ref, k_ref, v_ref, *rest, group):
    dilated = _regrouped_dilations()
    hbm_refs = rest[:len(dilated)]
    bias_ref, o_ref, m_ref, l_ref, u_ref, s0_ref, s1_ref, grp_ref, y_ref, sem = rest[len(dilated):]
    seq = q_ref.shape[0]
    width = hbm_refs[0].shape[-1] // 3
    b = pl.program_id(0)
    pair = pl.program_id(1)
    lane = lax.broadcasted_iota(jnp.int32, (1, LANES), 1)
    head_mask = (lane < HEAD_DIM, lane >= HEAD_DIM)
    nt_dims = (((1,), (1,)), ((), ()))
    ngroups = seq // BLOCK // group
    assert ngroups % 2 == 0

    def regroup_copy(n, a, r):
        rows = seq // dilated[n]
        cols = pl.ds(pl.multiple_of(a * width + pair * LANES, LANES), LANES)
        return pltpu.make_async_copy(hbm_refs[n].at[b, :, r, cols],
                                     grp_ref.at[n, a, pl.ds(r * rows, rows), :], sem.at[n, a])

    for n, dil in enumerate(dilated):
        for a in range(3):
            for r in range(dil):
                regroup_copy(n, a, r).start()

    for g, (_, dil) in enumerate(DA_PAIRS):
        nb = seq // dil // BLOCK
        srcs = (q_ref, k_ref, v_ref)
        if dil in dilated:
            n = dilated.index(dil)
            for a in range(3):
                for r in range(dil):
                    regroup_copy(n, a, r).wait()
            srcs = tuple(grp_ref.at[n, a] for a in range(3))

        def block_rows(blk):
            r = blk // nb
            j = blk % nb
            seq_rows = lambda jj: pl.ds(r + jj * (BLOCK * dil), BLOCK, stride=dil)
            own = pl.ds(pl.multiple_of(blk * BLOCK, BLOCK), BLOCK)
            prev = pl.ds(pl.multiple_of(jnp.maximum(blk - 1, 0) * BLOCK, BLOCK), BLOCK)
            if dil <= MERGE_DIL:
                out = own
            else:
                sub = dil // MERGE_DIL
                out = pl.ds((r % MERGE_DIL) * (seq // MERGE_DIL) + sub * BLOCK * j
                            + r // MERGE_DIL, BLOCK, stride=sub)
            if dil > 1 and dil not in dilated:
                own, prev = seq_rows(j), seq_rows(jnp.maximum(j - 1, 0))
            return own, prev, out, 2 * g + jnp.where(j == 0, 1, 0)

        def logits(grp, s_ref):
            for t in range(group):
                own, prev, _, table = block_rows(grp * group + t)
                q = srcs[0][own, :] * (1.0 / math.sqrt(HEAD_DIM))
                k2 = jnp.concatenate([srcs[1][prev, :], srcs[1][own, :]], axis=0).astype(BF16)
                for h in range(HEADS_PER_TILE):
                    qh = jnp.where(head_mask[h], q, 0.0).astype(BF16)
                    s = lax.dot_general(qh, k2, nt_dims, preferred_element_type=F32)
                    s_ref[HEADS_PER_TILE * t + h] = s + bias_ref[table, h]

        def finish(grp, s_ref):
            for t in range(group):
                own, prev, out, _ = block_rows(grp * group + t)
                v2 = jnp.concatenate([srcs[2][prev, :], srcs[2][own, :]], axis=0).astype(BF16)
                v2 = jnp.concatenate([v2, jnp.ones(v2.shape, BF16)], axis=1)
                stats = []
                for h in range(HEADS_PER_TILE):
                    s = s_ref[HEADS_PER_TILE * t + h]
                    m = jnp.max(s, axis=-1, keepdims=True)
                    p = jnp.exp(s - m)
                    pv = jnp.dot(p.astype(BF16), v2, preferred_element_type=F32)
                    stats.append((jnp.broadcast_to(m, (BLOCK, LANES)), pv[:, LANES:],
                                  pv[:, :LANES]))
                both = [jnp.where(head_mask[0], x0, x1) for x0, x1 in zip(*stats)]
                m_ref[g, out, :] = both[0]
                l_ref[g, out, :] = both[1]
                u_ref[g, out, :] = both[2]

        def body(i, carry):
            logits(2 * i + 1, s1_ref)
            finish(2 * i, s0_ref)
            logits(jnp.minimum(2 * i + 2, ngroups - 1), s0_ref)
            finish(2 * i + 1, s1_ref)
            return carry

        logits(jnp.int32(0), s0_ref)
        lax.fori_loop(0, ngroups // 2, body, 0)

    nbm = seq // MERGE_DIL // BLOCK

    def merge(c, carry):
        own = pl.ds(pl.multiple_of(c * BLOCK, BLOCK), BLOCK)
        in_seq = pl.ds(c // nbm + (c % nbm) * (BLOCK * MERGE_DIL), BLOCK, stride=MERGE_DIL)
        rows = [in_seq] + [own] * (len(DA_PAIRS) - 1)
        ms = [m_ref[g, rows[g], :] for g in range(len(DA_PAIRS))]
        top = functools.reduce(jnp.maximum, ms)
        wgts = [jnp.exp(m - top) for m in ms]
        num = sum(w * u_ref[g, rows[g], :] for g, w in enumerate(wgts))
        den = sum(w * l_ref[g, rows[g], :] for g, w in enumerate(wgts))
        y_ref[in_seq, :] = num / den
        return carry

    lax.fori_loop(0, seq // BLOCK, merge, 0)

    def emit(c, carry):
        rs = pl.ds(pl.multiple_of(c * BLOCK, BLOCK), BLOCK)
        o_ref[rs, :] = y_ref[rs, :].astype(o_ref.dtype)
        return carry

    lax.fori_loop(0, seq // BLOCK, emit, 0)


def dilated_attention(qkv, bias, batch, seq, group=4):
    n = qkv.shape[0]
    width = qkv.shape[1] // 3
    nhp = width // LANES
    dilated = _regrouped_dilations()
    blk = lambda off: pl.BlockSpec((seq, LANES), lambda b, p: (b, off + p))
    logits_buf = pltpu.VMEM((group * HEADS_PER_TILE, BLOCK, 2 * BLOCK), F32)
    by_residue = [qkv.reshape(batch, seq // dil, dil, qkv.shape[1]) for dil in dilated]
    return pl.pallas_call(
        functools.partial(_dilated_attn_kernel, group=group),
        grid=(batch, nhp),
        in_specs=[blk(0), blk(nhp), blk(2 * nhp)]
        + [pl.BlockSpec(memory_space=pl.ANY)] * len(dilated)
        + [pl.BlockSpec((bias.shape[0], HEADS_PER_TILE, BLOCK, 2 * BLOCK),
                        lambda b, p: (0, p, 0, 0))],
        out_specs=pl.BlockSpec((seq, LANES), lambda b, p: (b, p)),
        out_shape=jax.ShapeDtypeStruct((n, width), BF16),
        scratch_shapes=[pltpu.VMEM((len(DA_PAIRS), seq, LANES), F32)] * 3 + [logits_buf] * 2
        + [pltpu.VMEM((len(dilated), 3, seq, LANES), F32),
           pltpu.VMEM((seq, LANES), F32),
           pltpu.SemaphoreType.DMA((len(dilated), 3))],
        compiler_params=_cparams("parallel", "parallel"),
        name="dilated_attention",
    )(qkv, qkv, qkv, *by_residue, bias)


def _conformer_conv_kernel(a_ref, gate_ref, w_ref, b_ref, lg_ref, lb_ref, o_ref,
                           pad_ref, sh_ref, wb_ref, *, taps, halo, chunk):
    ts = o_ref.shape[0]

    @pl.when(pl.program_id(1) == 0)
    def _():
        pad_ref[0:halo, :] = jnp.zeros((halo, pad_ref.shape[1]), F32)

    @pl.when(pl.program_id(1) != 0)
    def _():
        pad_ref[0:halo, :] = pad_ref[ts:ts + halo, :]

    pad_ref[halo:, :] = a_ref[...].astype(F32) * jax.nn.sigmoid(gate_ref[...].astype(F32))

    lead = halo - (taps - 1)
    n_a = [len(range(b, taps, SUBLANES)) for b in range(SUBLANES)]
    width = o_ref.shape[1]
    for b in range(SUBLANES):
        groups = ts // SUBLANES + n_a[b] - 1
        win = pad_ref[lead + b:lead + b + groups * SUBLANES, :]
        sh_ref[b, 0:groups] = win.reshape(groups, SUBLANES, width)
    for j in range(taps):
        wb_ref[j] = jnp.broadcast_to(w_ref[j:j + 1, :], (SUBLANES, width))

    cg = chunk // SUBLANES
    for c0 in range(0, ts, chunk):
        acc = jnp.zeros((cg, SUBLANES, width), F32)
        for b in range(SUBLANES):
            for a in range(n_a[b]):
                g0 = c0 // SUBLANES + a
                acc = acc + wb_ref[SUBLANES * a + b][None] * sh_ref[b, g0:g0 + cg]
        y = acc.reshape(chunk, width) + b_ref[...]
        mu = jnp.mean(y, axis=-1, keepdims=True)
        yc = y - mu
        var = jnp.mean(yc * yc, axis=-1, keepdims=True)
        z = yc * lax.rsqrt(var + EPS) * lg_ref[...] + lb_ref[...]
        o_ref[c0:c0 + chunk, :] = (z * jax.nn.sigmoid(z)).astype(o_ref.dtype)


def conformer_conv(p, w_cc, b_cc, ln_g, ln_b, batch, seq, ts=512, chunk=32):
    n = p.shape[0]
    taps, width = w_cc.shape
    halo = -(-(taps - 1) // SUBLANES) * SUBLANES
    nt = seq // ts
    vec = lambda v: v.reshape(1, width)
    vspec = pl.BlockSpec((1, width), lambda b, t: (0, 0))
    return pl.pallas_call(
        functools.partial(_conformer_conv_kernel, taps=taps, halo=halo, chunk=chunk),
        grid=(batch, nt),
        in_specs=[pl.BlockSpec((ts, width), lambda b, t: (b * nt + t, 0)),
                  pl.BlockSpec((ts, width), lambda b, t: (b * nt + t, 1)),
                  pl.BlockSpec(w_cc.shape, lambda b, t: (0, 0)),
                  vspec, vspec, vspec],
        out_specs=pl.BlockSpec((ts, width), lambda b, t: (b * nt + t, 0)),
        out_shape=jax.ShapeDtypeStruct((n, width), BF16),
        scratch_shapes=[pltpu.VMEM((ts + halo, width), F32),
                        pltpu.VMEM((SUBLANES, (ts + halo) // SUBLANES, SUBLANES, width), F32),
                        pltpu.VMEM((taps, SUBLANES, width), F32)],
        compiler_params=_cparams("arbitrary", "arbitrary"),
        name="conformer_conv",
    )(p, p, w_cc, vec(b_cc), vec(ln_g), vec(ln_b))


def _suffix_sum_matrix():
    j = np.arange(BLOCK)[:, None]
    s = np.arange(BLOCK)[None, :]
    half = -np.concatenate([(j >= s).astype(np.float32), np.ones((BLOCK, BLOCK), np.float32)],
                           axis=1)
    return np.concatenate([half, half], axis=0)


def _stick_breaking_kernel(q_ref, k_ref, v_ref, u_ref, o_ref, qm_ref, run_ref, acc_ref, cat_ref,
                           *, fixed_steps):
    tq = q_ref.shape[0]
    nsub = tq // BLOCK
    qi = pl.program_id(2)
    lane = lax.broadcasted_iota(jnp.int32, (1, LANES), 1)
    head_mask = (lane < HEAD_DIM, lane >= HEAD_DIM)
    nt_dims = (((1,), (1,)), ((), ()))
    rows2 = HEADS_PER_TILE * BLOCK
    row = lax.broadcasted_iota(jnp.int32, (rows2, BLOCK), 0)
    col = lax.broadcasted_iota(jnp.int32, (rows2, BLOCK), 1)
    before = col < jnp.bitwise_and(row, BLOCK - 1)

    for r in range(nsub):
        q = q_ref[r * BLOCK:(r + 1) * BLOCK, :].astype(F32) * (1.0 / math.sqrt(HEAD_DIM))
        qm_ref[r] = jnp.concatenate(
            [jnp.where(head_mask[h], q, 0.0) for h in range(HEADS_PER_TILE)], axis=0).astype(BF16)
    run_ref[...] = jnp.zeros(run_ref.shape, F32)

    sign_bit = jnp.int32(-2 ** 31)

    def sweep(steps, first):
        plan = []
        for slot, d in enumerate(steps):
            kbs = [qi * nsub + r - d for r in range(nsub)]
            kss = [pl.ds(pl.multiple_of(jnp.maximum(kb, 0) * BLOCK, BLOCK), BLOCK) for kb in kbs]
            zs = [lax.dot_general(qm_ref[r], k_ref[kss[r], :], nt_dims,
                                  preferred_element_type=F32) for r in range(nsub)]
            plan.append((isinstance(d, int) and d == 0, kbs, kss, zs, cat_ref.at[slot]))
        all_sums = []
        for own_block, _, _, zs, cat in plan:
            for r, z in enumerate(zs):
                neg_abs = pltpu.bitcast(pltpu.bitcast(z, jnp.int32) | sign_bit, F32)
                neg_log_rest = jnp.maximum(z, 0.0) + jnp.log(1.0 + jnp.exp(neg_abs))
                if own_block:
                    neg_log_rest = jnp.where(before, neg_log_rest, 0.0)
                hi = neg_log_rest.astype(BF16)
                lo = (neg_log_rest - hi.astype(F32)).astype(BF16)
                cat[r * rows2:(r + 1) * rows2, :] = jnp.concatenate([hi, lo], axis=1)
            all_sums.append(jnp.dot(cat[...], u_ref[...], preferred_element_type=F32))
        weights = [[] for _ in range(nsub)]
        values = [[] for _ in range(nsub)]
        for (own_block, kbs, kss, zs, _), sums in zip(plan, all_sums):
            for r in range(nsub):
                sub = sums[r * rows2:(r + 1) * rows2]
                run = run_ref[r]
                a = jnp.exp(zs[r] + sub[:, 0:BLOCK] + run)
                if own_block:
                    a = jnp.where(before, a, 0.0)
                weights[r].append(a.astype(BF16))
                run_ref[r] = run + sub[:, BLOCK:2 * BLOCK]
                v_blk = v_ref[kss[r], :]
                values[r].append(v_blk if own_block else
                                 jnp.where(kbs[r] < 0, jnp.zeros_like(v_blk), v_blk))
        for r in range(nsub):
            out = jnp.dot(jnp.concatenate(weights[r], axis=1), jnp.concatenate(values[r], axis=0),
                          preferred_element_type=F32)
            acc_ref[r] = out if first else acc_ref[r] + out

    def still_open(d):
        return jnp.logical_and(d < (qi + 1) * nsub, jnp.max(run_ref[...]) > SB_EXIT)

    def body(c):
        d, _ = c
        sweep([d], False)
        return d + 1, still_open(d + 1)

    sweep(list(range(fixed_steps)), True)
    lax.while_loop(lambda c: c[1], body, (jnp.int32(fixed_steps), still_open(fixed_steps)))
    for r in range(nsub):
        o_ref[r * BLOCK:(r + 1) * BLOCK, :] = jnp.where(
            head_mask[0], acc_ref[r, 0:BLOCK, :], acc_ref[r, BLOCK:rows2, :]).astype(o_ref.dtype)


def stick_breaking(p, col0, batch, seq, tq=2048, fixed_steps=3):
    n = p.shape[0]
    width = (p.shape[1] - col0) // 3
    nhp = width // LANES
    c0 = col0 // LANES
    nq = seq // tq
    u = jnp.asarray(_suffix_sum_matrix(), BF16)
    return pl.pallas_call(
        functools.partial(_stick_breaking_kernel, fixed_steps=fixed_steps),
        grid=(batch, nhp, nq),
        in_specs=[pl.BlockSpec((tq, LANES), lambda b, p_, i: (b * nq + i, c0 + p_)),
                  pl.BlockSpec((seq, LANES), lambda b, p_, i: (b, c0 + nhp + p_)),
                  pl.BlockSpec((seq, LANES), lambda b, p_, i: (b, c0 + 2 * nhp + p_)),
                  pl.BlockSpec(u.shape, lambda b, p_, i: (0, 0))],
        out_specs=pl.BlockSpec((tq, LANES), lambda b, p_, i: (b * nq + i, p_)),
        out_shape=jax.ShapeDtypeStruct((n, width), BF16),
        scratch_shapes=[pltpu.VMEM((tq // BLOCK, HEADS_PER_TILE * BLOCK, LANES), BF16),
                        pltpu.VMEM((tq // BLOCK, HEADS_PER_TILE * BLOCK, LANES), F32),
                        pltpu.VMEM((tq // BLOCK, HEADS_PER_TILE * BLOCK, LANES), F32),
                        pltpu.VMEM((fixed_steps, HEADS_PER_TILE * tq, 2 * BLOCK), BF16)],
        compiler_params=_cparams("parallel", "parallel", "arbitrary"),
        name="stick_breaking",
    )(p, p, p, u)


def _layer_tail_kernel(ya_ref, yb_ref, x_ref, wo_ref, g_ref, wup_ref, wc_ref, bc_ref, wd_ref,
                       *rest, tiles_per_seq, fc, col_chunk, has_next, parts):
    if has_next:
        gn_ref, win_ref, o_ref, *p_refs, act_ref, pad_ref, carry_ref, h_ref = rest
    else:
        o_ref, act_ref, pad_ref, carry_ref, h_ref = rest
        p_refs = []
    tm = x_ref.shape[0]
    d_ff = wd_ref.shape[0]
    wa_rows = ya_ref.shape[1]

    @pl.when(pl.program_id(0) % tiles_per_seq == 0)
    def _():
        carry_ref[...] = jnp.zeros(carry_ref.shape, F32)

    tp = tm // parts
    row_parts = [slice(r0, r0 + tp) for r0 in range(0, tm, tp)]
    mixes = [jnp.dot(ya_ref[rows, :], wo_ref[0:wa_rows, :], preferred_element_type=F32)
             + jnp.dot(yb_ref[rows, :], wo_ref[wa_rows:, :], preferred_element_type=F32)
             for rows in row_parts]
    xs = [x_ref[rows, :] + _rms(mix, g_ref[0:1, :]) for rows, mix in zip(row_parts, mixes)]
    for rows, xp in zip(row_parts, xs):
        h_ref[rows, :] = _rms(xp, g_ref[1:2, :]).astype(BF16)

    for part, rows in enumerate(row_parts):
        h = h_ref[rows, :]
        for n, c0 in enumerate(range(0, d_ff, fc)):
            cols = slice(c0, c0 + fc)
            gate = jnp.dot(h, wup_ref[:, cols], preferred_element_type=F32)
            up = jnp.dot(h, wup_ref[:, d_ff + c0:d_ff + c0 + fc], preferred_element_type=F32)
            pad = pad_ref.at[(part * (d_ff // fc) + n) % pad_ref.shape[0]]
            pad[0:SUBLANES, :] = carry_ref[:, cols]
            pad[SUBLANES:, :] = gate
            carry_ref[:, cols] = gate[tp - SUBLANES:, :]
            conv = (wc_ref[2:3, cols] * gate
                    + wc_ref[1:2, cols] * pad[SUBLANES - 1:SUBLANES - 1 + tp, :]
                    + wc_ref[0:1, cols] * pad[SUBLANES - 2:SUBLANES - 2 + tp, :]
                    + bc_ref[:, cols])
            act_ref[rows, cols] = (conv * jax.nn.sigmoid(conv) * up).astype(BF16)
    outs = [jnp.dot(act_ref[rows, :], wd_ref[...], preferred_element_type=F32)
            for rows in row_parts]
    for rows, xp, out in zip(row_parts, xs, outs):
        xp = xp + _rms(out, g_ref[2:3, :])
        o_ref[rows, :] = xp
        if has_next:
            h_ref[rows, :] = _rms(xp, gn_ref[...]).astype(BF16)

    if has_next:
        for rows in row_parts:
            hn = h_ref[rows, :]
            col = 0
            for p_ref in p_refs:
                width = p_ref.shape[1]
                for c in range(0, width, col_chunk):
                    p_ref[rows, c:c + col_chunk] = jnp.dot(
                        hn, win_ref[:, col + c:col + c + col_chunk],
                        preferred_element_type=F32).astype(p_ref.dtype)
                col += width


def layer_tail(ya, yb, x, w_out, gains, w_up, w_conv, b_conv, w_down, seq,
               next_gain=None, next_w_in=None, next_outs=(), tm=512, fc=256, col_chunk=512,
               parts=2):
    n, d = x.shape
    d_ff = w_down[0].shape[1]
    has_next = next_w_in is not None
    assert d_ff % fc == 0 and seq % tm == 0
    const = lambda shape: pl.BlockSpec(shape, lambda i: (0, 0), pipeline_mode=pl.Buffered(1))
    row = lambda width: pl.BlockSpec((tm, width), lambda i: (i, 0))
    in_specs = [row(ya.shape[1]), row(yb.shape[1]), row(d), _layer_weight_spec(*w_out),
                const((3, d)), _layer_weight_spec(*w_up), const(w_conv.shape), const((1, d_ff)),
                _layer_weight_spec(*w_down)]
    args = [ya, yb, x, w_out[0], gains, w_up[0], w_conv, b_conv.reshape(1, d_ff), w_down[0]]
    out_specs = [row(d)]
    out_shape = [jax.ShapeDtypeStruct((n, d), F32)]
    if has_next:
        in_specs += [const((1, d)), _layer_weight_spec(*next_w_in)]
        args += [next_gain.reshape(1, d), next_w_in[0]]
        out_specs += [row(wd) for wd, _ in next_outs]
        out_shape += [jax.ShapeDtypeStruct((n, wd), dt) for wd, dt in next_outs]
    return pl.pallas_call(
        functools.partial(_layer_tail_kernel, tiles_per_seq=seq // tm, fc=fc,
                          col_chunk=col_chunk, has_next=has_next, parts=parts),
        grid=(n // tm,),
        in_specs=in_specs,
        out_specs=out_specs,
        out_shape=out_shape,
        scratch_shapes=[pltpu.VMEM((tm, d_ff), BF16),
                        pltpu.VMEM((2, tm // parts + SUBLANES, fc), F32),
                        pltpu.VMEM((SUBLANES, d_ff), F32),
                        pltpu.VMEM((tm, d), BF16)],
        compiler_params=_cparams("arbitrary"),
        name="layer_tail",
    )(*args)


def kernel(x, norm_g, rel_bias, w_in_even, w_out_even, w_sc, w_in_odd, w_out_odd, w_cc, b_cc,
           ln_cc_g, ln_cc_b, w_up, w_ffn_conv, b_ffn_conv, w_down):
    batch, seq, d = x.shape
    depth = norm_g.shape[0]
    sc_width = w_sc.shape[2]
    cc_width = w_cc.shape[2]
    xf = x.reshape(batch * seq, d)
    bias = bias_tables(rel_bias)

    stacks = {"in_even": w_in_even, "in_odd": w_in_odd, "out_even": w_out_even,
              "out_odd": w_out_odd, "up": w_up, "down": w_down}
    stacks = {name: w.astype(BF16) for name, w in stacks.items()}

    def in_proj(layer):
        if layer % 2 == 0:
            conv_cols = 3 * sc_width
            w_in = (stacks["in_even"], layer // 2)
            return w_in, [(conv_cols, BF16), (w_in[0].shape[2] - conv_cols, F32)]
        w_in = (stacks["in_odd"], layer // 2)
        return w_in, [(w_in[0].shape[2], BF16)]

    w_in, outs = in_proj(0)
    proj = norm_matmul(xf, norm_g[0, 0], *w_in, outs)
    for layer in range(depth):
        i = layer // 2
        if layer % 2 == 0:
            pa, qkv = proj
            y_a = short_conv(pa, w_sc[i], batch, seq)
            y_b = dilated_attention(qkv, bias, batch, seq)
            w_out = (stacks["out_even"], i)
        else:
            (p,) = proj
            y_a = conformer_conv(p, w_cc[i], b_cc[i], ln_cc_g[i], ln_cc_b[i], batch, seq)
            y_b = stick_breaking(p, 2 * cc_width, batch, seq)
            w_out = (stacks["out_odd"], i)
        nxt = {}
        if layer + 1 < depth:
            w_in, outs = in_proj(layer + 1)
            nxt = dict(next_gain=norm_g[layer + 1, 0], next_w_in=w_in, next_outs=outs)
        xf, *proj = layer_tail(y_a, y_b, xf, w_out, norm_g[layer, 1:4], (stacks["up"], layer),
                               w_ffn_conv[layer], b_ffn_conv[layer], (stacks["down"], layer),
                               seq, **nxt)
    return xf.reshape(batch, seq, d)
```
